```python
import jax
import jax.numpy as jnp
from jax import lax
import numpy as np

D_MODEL = 1024
BATCH = 8
SEQ = 2048
DEPTH = 2
DEC_BATCH = 128
DEC_SEQ = 8
PAST_LEN = 16384
PAGE_SIZE = 128

N_META = 16
CONV_W = 4
CHUNK = 64
LRU_W = 256
LRU_BLOCKS = 4
LRU_BD = LRU_W // LRU_BLOCKS
LRU_C = 8.0
ML_H = 4
ML_D = 96
ML_W = ML_H * ML_D
GLA_H = 4
GLA_DK = 48
GLA_DV = 96
GLA_KW = GLA_H * GLA_DK
GLA_VW = GLA_H * GLA_DV
GLA_RANK = 16
GLA_TAU = 16.0
MIX_W = LRU_W + ML_W + GLA_VW
IN_SIZES = (LRU_W, LRU_W, ML_W, ML_W, GLA_KW, GLA_KW, GLA_VW, GLA_VW, GLA_RANK)
D_IN = 2 * LRU_W + 2 * ML_W + 2 * GLA_KW + 2 * GLA_VW + GLA_RANK
D_FF = 2816
EPS = 1e-6

kernel_name = 'hymba_style_rglru_mlstm_gla_macaron_step'


def _split_points():
    pts, acc = [], 0
    for s in IN_SIZES[:-1]:
        acc += s
        pts.append(acc)
    return pts


def rmsnorm(x, g):
    x32 = x.astype(jnp.float32)
    y = x32 * lax.rsqrt(jnp.mean(x32 * x32, axis=-1, keepdims=True) + EPS)
    return (y * g.astype(jnp.float32)).astype(x.dtype)


def head_rmsnorm(h, g):
    Bsz, L, H, Dh = h.shape
    y = h * lax.rsqrt(jnp.mean(h * h, axis=-1, keepdims=True) + EPS)
    return y.reshape(Bsz, L, H * Dh) * g.astype(jnp.float32)


def swiglu(x, w1, w3, w2):
    return (jax.nn.silu(x @ w1) * (x @ w3)) @ w2


def causal_conv(u, buf, w, b):
    L = u.shape[1]
    ext = jnp.concatenate([buf.astype(u.dtype), u], axis=1)
    out = ext[:, 0:L] * w[0]
    for j in range(1, CONV_W):
        out = out + ext[:, j:j + L] * w[j]
    return out + b, ext[:, ext.shape[1] - (CONV_W - 1):]


def rglru(x, h0, wa, ba, wx, bx, lam):
    Bsz, L, _ = x.shape
    xb = x.reshape(Bsz, L, LRU_BLOCKS, LRU_BD)
    r = jax.nn.sigmoid(jnp.einsum('blni,nij->blnj', xb, wa).reshape(Bsz, L, LRU_W) + ba)
    ig = jax.nn.sigmoid(jnp.einsum('blni,nij->blnj', xb, wx).reshape(Bsz, L, LRU_W) + bx)
    log_a = -LRU_C * r * jax.nn.softplus(-lam)
    a = jnp.exp(log_a)
    bterm = jnp.sqrt(-jnp.expm1(2.0 * log_a)) * (ig * x)
    bterm = bterm.at[:, 0].add(a[:, 0] * h0)

    def combine(left, right):
        a1, b1 = left
        a2, b2 = right
        return a1 * a2, a2 * b1 + b2

    _, hs = lax.associative_scan(combine, (a, bterm), axis=1)
    return hs, hs[:, -1]


def mlstm_chunk(state, inp):
    C = state[0].astype(jnp.float32)
    n = state[1].astype(jnp.float32)
    m = state[2].astype(jnp.float32)
    q, k, v, li, lf = inp
    T = q.shape[1]
    causal = jnp.tril(jnp.ones((T, T), dtype=bool))
    qs = q * (ML_D ** -0.5)
    b = jnp.cumsum(lf, axis=1).transpose(0, 2, 1)
    lih = li.transpose(0, 2, 1)
    dmat = jnp.where(causal, b[..., :, None] - b[..., None, :] + lih[..., None, :], -jnp.inf)
    m_inter = b + m[..., None]
    m_t = jnp.maximum(m_inter, jnp.max(dmat, axis=-1))
    w_intra = jnp.exp(dmat - m_t[..., None])
    w_inter = jnp.exp(m_inter - m_t)
    s = jnp.einsum('bthd,bshd->bhts', qs, k) * w_intra
    num = (jnp.einsum('bhts,bshe->bthe', s, v)
           + jnp.einsum('bthd,bhde->bthe', qs, C) * w_inter.transpose(0, 2, 1)[..., None])
    den = jnp.sum(s, axis=-1) + w_inter * jnp.einsum('bthd,bhd->bht', qs, n)
    denom = jnp.maximum(jnp.abs(den), jnp.exp(-m_t)).transpose(0, 2, 1)[..., None]
    h = num / denom
    m_new = m_t[..., -1]
    wk = jnp.exp(b[..., -1:] - b + lih - m_new[..., None])
    decay = jnp.exp(b[..., -1] + m - m_new)
    C_new = decay[..., None, None] * C + jnp.einsum('bhs,bshd,bshe->bhde', wk, k, v)
    n_new = decay[..., None] * n + jnp.einsum('bhs,bshd->bhd', wk, k)
    return (C_new, n_new, m_new), h


def gla_chunk(S, inp):
    S = S.astype(jnp.float32)
    q, k, v, lg = inp
    T = q.shape[1]
    causal = jnp.tril(jnp.ones((T, T), dtype=bool))
    bc = jnp.cumsum(lg, axis=1)
    diff = jnp.where(causal[None, :, :, None, None], bc[:, :, None] - bc[:, None, :], -jnp.inf)
    A = jnp.einsum('bthk,bshk,btshk->bhts', q, k, jnp.exp(diff))
    o = jnp.einsum('bhts,bshv->bthv', A, v) + jnp.einsum('bthk,bhkv->bthv', q * jnp.exp(bc), S)
    last = bc[:, -1]
    S_new = jnp.exp(last)[..., None] * S + jnp.einsum('bshk,bshv->bhkv', k * jnp.exp(last[:, None] - bc), v)
    return S_new, o


def run_chunked(chunk_fn, state, inputs, prompt):
    if not prompt:
        return chunk_fn(state, inputs)
    state, out_meta = chunk_fn(state, tuple(a[:, :N_META] for a in inputs))
    rest = tuple(a[:, N_META:] for a in inputs)
    Bsz, L = rest[0].shape[0], rest[0].shape[1]
    nc = L // CHUNK
    chunks = tuple(a.reshape((Bsz, nc, CHUNK) + a.shape[2:]).swapaxes(0, 1) for a in rest)
    state, outs = lax.scan(chunk_fn, state, chunks)
    out_rest = outs.swapaxes(0, 1).reshape((Bsz, L) + outs.shape[3:])
    return state, jnp.concatenate([out_meta, out_rest], axis=1)


def mix(h, st, lp, prompt):
    f32 = jnp.float32
    lru_h, lru_conv, m_C, m_n, m_m, m_conv, g_S = st
    Bsz, L, _ = h.shape
    z = h @ lp['w_in']
    u_r, g_r, u_m, z_m, q_g, k_g, v_g, g_g, a_lr = jnp.split(z, _split_points(), axis=-1)
    xr, lru_conv_new = causal_conv(u_r, lru_conv, lp['lru_conv_w'], lp['lru_conv_b'])
    hr, lru_h_new = rglru(xr.astype(f32), lru_h.astype(f32), lp['lru_wa'], lp['lru_ba'],
                          lp['lru_wx'], lp['lru_bx'], lp['lru_lambda'].astype(f32))
    y_r = jax.nn.gelu(g_r.astype(f32)) * hr
    cm, m_conv_new = causal_conv(u_m, m_conv, lp['ml_conv_w'], lp['ml_conv_b'])
    cm = jax.nn.silu(cm.astype(f32))
    cmh = cm.reshape(Bsz, L, ML_H, ML_D)
    umh = u_m.astype(f32).reshape(Bsz, L, ML_H, ML_D)
    mq = jnp.einsum('blhd,hde->blhe', cmh, lp['ml_wq'])
    mk = jnp.einsum('blhd,hde->blhe', cmh, lp['ml_wk'])
    mv = jnp.einsum('blhd,hde->blhe', umh, lp['ml_wv'])
    gin = jnp.concatenate([mq.reshape(Bsz, L, ML_W), mk.reshape(Bsz, L, ML_W), mv.reshape(Bsz, L, ML_W)], axis=-1)
    gates = gin @ lp['ml_w_if'] + lp['ml_b_if']
    li = gates[..., :ML_H]
    lf = jax.nn.log_sigmoid(gates[..., ML_H:])
    (m_C_new, m_n_new, m_m_new), hm = run_chunked(
        mlstm_chunk, (m_C.astype(f32), m_n.astype(f32), m_m.astype(f32)), (mq, mk, mv, li, lf), prompt)
    y_m = jax.nn.sigmoid(z_m.astype(f32)) * (head_rmsnorm(hm, lp['ml_norm_g']) + lp['ml_skip'] * cm)
    gq = q_g.astype(f32).reshape(Bsz, L, GLA_H, GLA_DK) * (GLA_DK ** -0.5)
    gk = k_g.astype(f32).reshape(Bsz, L, GLA_H, GLA_DK)
    gv = v_g.astype(f32).reshape(Bsz, L, GLA_H, GLA_DV)
    lg = (jax.nn.log_sigmoid(a_lr.astype(f32) @ lp['gla_w_up'] + lp['gla_b_up']) / GLA_TAU).reshape(Bsz, L, GLA_H, GLA_DK)
    g_S_new, og = run_chunked(gla_chunk, g_S.astype(f32), (gq, gk, gv, lg), prompt)
    y_g = head_rmsnorm(og, lp['gla_norm_g']) * jax.nn.silu(g_g.astype(f32))
    y = jnp.concatenate([y_r, y_m, y_g], axis=-1).astype(h.dtype) @ lp['w_out']
    return y, (lru_h_new, lru_conv_new, m_C_new, m_n_new, m_m_new, m_conv_new, g_S_new)


def trunk(x, states, lps, prompt):
    new_states = []
    for l in range(DEPTH):
        lp = lps[l]
        x = x + 0.5 * swiglu(rmsnorm(x, lp['ffn1_norm_g']), lp['ffn1_w1'], lp['ffn1_w3'], lp['ffn1_w2'])
        y, st = mix(rmsnorm(x, lp['mix_norm_g']), states[l], lp, prompt)
        x = x + y
        x = x + 0.5 * swiglu(rmsnorm(x, lp['ffn2_norm_g']), lp['ffn2_w1'], lp['ffn2_w3'], lp['ffn2_w2'])
        new_states.append(st)
    stacked = [jnp.stack([s[i] for s in new_states]) for i in range(7)]
    return x, stacked


def setup_inputs(seed: int = 0) -> dict:
    key = jax.random.key(seed)
    ks = iter(jax.random.split(key, 64))
    f32 = jnp.float32

    def nrm(shape, scale):
        return jax.random.normal(next(ks), shape, f32) * scale

    def gain(shape):
        return 1.0 + nrm(shape, 0.02)

    u = jax.random.uniform(next(ks), (DEPTH, LRU_W), f32, minval=0.9, maxval=0.999)
    s = u ** (1.0 / LRU_C)
    lru_lambda = jnp.log(s) - jnp.log1p(-s)
    ml_b_if = jnp.concatenate([nrm((DEPTH, ML_H), 0.1),
                               jax.random.uniform(next(ks), (DEPTH, ML_H), f32, minval=3.0, maxval=6.0)], axis=-1)
    return {
        'x_prompt': nrm((BATCH, SEQ, D_MODEL), 1.0),
        'x_sample': nrm((DEC_BATCH, DEC_SEQ, D_MODEL), 1.0),
        'state_lru_h': nrm((DEPTH, DEC_BATCH, LRU_W), 0.5),
        'state_lru_conv': nrm((DEPTH, DEC_BATCH, CONV_W - 1, LRU_W), 1.0),
        'state_mlstm_C': nrm((DEPTH, DEC_BATCH, ML_H, ML_D, ML_D), 1.0),
        'state_mlstm_n': jnp.abs(nrm((DEPTH, DEC_BATCH, ML_H, ML_D), 1.0)),
        'state_mlstm_m': nrm((DEPTH, DEC_BATCH, ML_H), 1.0),
        'state_mlstm_conv': nrm((DEPTH, DEC_BATCH, CONV_W - 1, ML_W), 1.0),
        'state_gla_S': nrm((DEPTH, DEC_BATCH, GLA_H, GLA_DK, GLA_DV), 0.3),
        'meta_tokens': nrm((N_META, D_MODEL), 1.0),
        'ffn1_norm_g': gain((DEPTH, D_MODEL)),
        'ffn1_w1': nrm((DEPTH, D_MODEL, D_FF), D_MODEL ** -0.5),
        'ffn1_w3': nrm((DEPTH, D_MODEL, D_FF), D_MODEL ** -0.5),
        'ffn1_w2': nrm((DEPTH, D_FF, D_MODEL), D_FF ** -0.5),
        'mix_norm_g': gain((DEPTH, D_MODEL)),
        'w_in': nrm((DEPTH, D_MODEL, D_IN), D_MODEL ** -0.5),
        'lru_conv_w': nrm((DEPTH, CONV_W, LRU_W), CONV_W ** -0.5),
        'lru_conv_b': nrm((DEPTH, LRU_W), 0.02),
        'lru_wa': nrm((DEPTH, LRU_BLOCKS, LRU_BD, LRU_BD), LRU_BD ** -0.5),
        'lru_ba': nrm((DEPTH, LRU_W), 0.02),
        'lru_wx': nrm((DEPTH, LRU_BLOCKS, LRU_BD, LRU_BD), LRU_BD ** -0.5),
        'lru_bx': nrm((DEPTH, LRU_W), 0.02),
        'lru_lambda': lru_lambda,
        'ml_conv_w': nrm((DEPTH, CONV_W, ML_W), CONV_W ** -0.5),
        'ml_conv_b': nrm((DEPTH, ML_W), 0.02),
        'ml_wq': nrm((DEPTH, ML_H, ML_D, ML_D), ML_D ** -0.5),
        'ml_wk': nrm((DEPTH, ML_H, ML_D, ML_D), ML_D ** -0.5),
        'ml_wv': nrm((DEPTH, ML_H, ML_D, ML_D), ML_D ** -0.5),
        'ml_w_if': nrm((DEPTH, 3 * ML_W, 2 * ML_H), (3 * ML_W) ** -0.5),
        'ml_b_if': ml_b_if,
        'ml_norm_g': gain((DEPTH, ML_W)),
        'ml_skip': gain((DEPTH, ML_W)),
        'gla_w_up': nrm((DEPTH, GLA_RANK, GLA_KW), GLA_RANK ** -0.5),
        'gla_b_up': nrm((DEPTH, GLA_KW), 0.1),
        'gla_norm_g': gain((DEPTH, GLA_VW)),
        'w_out': nrm((DEPTH, MIX_W, D_MODEL), MIX_W ** -0.5),
        'ffn2_norm_g': gain((DEPTH, D_MODEL)),
        'ffn2_w1': nrm((DEPTH, D_MODEL, D_FF), D_MODEL ** -0.5),
        'ffn2_w3': nrm((DEPTH, D_MODEL, D_FF), D_MODEL ** -0.5),
        'ffn2_w2': nrm((DEPTH, D_FF, D_MODEL), D_FF ** -0.5),
        'final_norm_g': gain((D_MODEL,)),
    }


def reference(x_prompt, x_sample, state_lru_h, state_lru_conv, state_mlstm_C, state_mlstm_n,
              state_mlstm_m, state_mlstm_conv, state_gla_S, meta_tokens,
              ffn1_norm_g, ffn1_w1, ffn1_w3, ffn1_w2, mix_norm_g, w_in,
              lru_conv_w, lru_conv_b, lru_wa, lru_ba, lru_wx, lru_bx, lru_lambda,
              ml_conv_w, ml_conv_b, ml_wq, ml_wk, ml_wv, ml_w_if, ml_b_if, ml_norm_g, ml_skip,
              gla_w_up, gla_b_up, gla_norm_g, w_out,
              ffn2_norm_g, ffn2_w1, ffn2_w3, ffn2_w2, final_norm_g):
    f32 = jnp.float32
    lps = [dict(ffn1_norm_g=ffn1_norm_g[l], ffn1_w1=ffn1_w1[l], ffn1_w3=ffn1_w3[l], ffn1_w2=ffn1_w2[l],
                mix_norm_g=mix_norm_g[l], w_in=w_in[l],
                lru_conv_w=lru_conv_w[l], lru_conv_b=lru_conv_b[l], lru_wa=lru_wa[l], lru_ba=lru_ba[l],
                lru_wx=lru_wx[l], lru_bx=lru_bx[l], lru_lambda=lru_lambda[l],
                ml_conv_w=ml_conv_w[l], ml_conv_b=ml_conv_b[l], ml_wq=ml_wq[l], ml_wk=ml_wk[l],
                ml_wv=ml_wv[l], ml_w_if=ml_w_if[l], ml_b_if=ml_b_if[l], ml_norm_g=ml_norm_g[l],
                ml_skip=ml_skip[l], gla_w_up=gla_w_up[l], gla_b_up=gla_b_up[l], gla_norm_g=gla_norm_g[l],
                w_out=w_out[l], ffn2_norm_g=ffn2_norm_g[l], ffn2_w1=ffn2_w1[l], ffn2_w3=ffn2_w3[l],
                ffn2_w2=ffn2_w2[l]) for l in range(DEPTH)]

    Bp = x_prompt.shape[0]
    meta = jnp.broadcast_to(meta_tokens.astype(x_prompt.dtype)[None], (Bp, N_META, D_MODEL))
    xp = jnp.concatenate([meta, x_prompt], axis=1)
    zero_state = (jnp.zeros((Bp, LRU_W), f32), jnp.zeros((Bp, CONV_W - 1, LRU_W), f32),
                  jnp.zeros((Bp, ML_H, ML_D, ML_D), f32), jnp.zeros((Bp, ML_H, ML_D), f32),
                  jnp.zeros((Bp, ML_H), f32), jnp.zeros((Bp, CONV_W - 1, ML_W), f32),
                  jnp.zeros((Bp, GLA_H, GLA_DK, GLA_DV), f32))
    xp, p_new = trunk(xp, [zero_state for _ in range(DEPTH)], lps, True)
    y_prompt = rmsnorm(xp, final_norm_g)[:, N_META:]

    s_states = [(state_lru_h[l], state_lru_conv[l], state_mlstm_C[l], state_mlstm_n[l],
                 state_mlstm_m[l], state_mlstm_conv[l], state_gla_S[l]) for l in range(DEPTH)]
    xs, s_new = trunk(x_sample, s_states, lps, False)
    y_sample = rmsnorm(xs, final_norm_g)

    return (y_prompt, y_sample,
            p_new[0], p_new[1], p_new[2], p_new[3], p_new[4], p_new[5], p_new[6],
            s_new[0], s_new[1], s_new[2], s_new[3], s_new[4], s_new[5], s_new[6])
```

```python
import functools

import jax
import jax.numpy as jnp
from jax import lax
from jax.experimental import pallas as pl
from jax.experimental.pallas import tpu as pltpu

f32 = jnp.float32
bf16 = jnp.bfloat16

D_MODEL = 1024
DEPTH = 2
N_META = 16
CONV_W = 4
CHUNK = 64
LRU_W = 256
LRU_BLOCKS = 4
LRU_C = 8.0
ML_H = 4
ML_D = 96
ML_W = ML_H * ML_D
GLA_H = 4
GLA_DK = 48
GLA_DV = 96
GLA_KW = GLA_H * GLA_DK
GLA_VW = GLA_H * GLA_DV
GLA_RANK = 16
GLA_TAU = 16.0
D_FF = 2816
EPS = 1e-6

SUBLANES = 8
LANES = 128
VMEM_LIMIT_BYTES = 56 * 1024 * 1024

FF_TILE = 256
N_FF_TILES = D_FF // FF_TILE
GLA_SUB = 16

Z_UR = 0
Z_GR = Z_UR + LRU_W
Z_UM = Z_GR + LRU_W
Z_ZM = Z_UM + ML_W
Z_QG = Z_ZM + ML_W
GLA_KP = 256
Z_KG = Z_QG + GLA_KP
Z_VG = Z_KG + GLA_KP
Z_GG = Z_VG + GLA_VW
Z_AL = Z_GG + GLA_VW
Z_W = Z_AL + LANES


def _dot(a, b):
    return jnp.dot(a, b, preferred_element_type=f32)


def _dot_nt(a, b):
    return lax.dot_general(a, b, (((1,), (1,)), ((), ())), preferred_element_type=f32)


def _dot_tn(a, b):
    return lax.dot_general(a, b, (((0,), (0,)), ((), ())), preferred_element_type=f32)


def _rms(x, g):
    return x * lax.rsqrt(jnp.mean(x * x, axis=-1, keepdims=True) + EPS) * g


def _ffn_body(x_ref, g_ref, w13_ref, w2_ref, gf_ref, o_ref, *, post_norm):
    x = x_ref[...]
    h = _rms(x, g_ref[...]).astype(bf16)

    def step(f, acc):
        ab = _dot(h, w13_ref[f])
        a = ab[:, :FF_TILE]
        b = ab[:, FF_TILE:]
        gated = (a * jax.nn.sigmoid(a) * b).astype(bf16)
        return acc + _dot(gated, w2_ref[f])

    acc = lax.fori_loop(0, N_FF_TILES, step, jnp.zeros(x.shape, f32))
    y = x + 0.5 * acc
    if post_norm:
        y = _rms(y, gf_ref[...])
    o_ref[...] = y


def _const_spec(shape):
    zeros = (0,) * len(shape)
    return pl.BlockSpec(shape, lambda *_: zeros, pipeline_mode=pl.Buffered(1))


def _ffn(x2d, g, w13, w2, gf, *, tm, post_norm):
    rows = x2d.shape[0]
    return pl.pallas_call(
        functools.partial(_ffn_body, post_norm=post_norm),
        out_shape=jax.ShapeDtypeStruct(x2d.shape, f32),
        grid=(rows // tm,),
        in_specs=[
            pl.BlockSpec((tm, D_MODEL), lambda i: (i, 0)),
            _const_spec(g.shape),
            _const_spec(w13.shape),
            _const_spec(w2.shape),
            _const_spec(gf.shape),
        ],
        out_specs=pl.BlockSpec((tm, D_MODEL), lambda i: (i, 0)),
        compiler_params=pltpu.CompilerParams(
            dimension_semantics=("arbitrary",), vmem_limit_bytes=VMEM_LIMIT_BYTES),
        name="ffn",
    )(x2d, g, w13, w2, gf)


def _row_time(shape, period):
    return lax.broadcasted_iota(jnp.int32, shape, 0) % period


def _seg_cumsum(x, period):
    t = _row_time(x.shape, period)
    s = 1
    while s < period:
        x = x + jnp.where(t >= s, pltpu.roll(x, s, 0), 0.0)
        s *= 2
    return x


def _causal_conv(ext_ref, u, w_ref, b_ref, sb, t_len):
    width = u.shape[-1]
    ext_ref[:, SUBLANES:SUBLANES + t_len, :] = u.reshape(sb, t_len, width)
    w = w_ref[...]
    base = SUBLANES - (CONV_W - 1)
    out = b_ref[...] + ext_ref[:, base:base + t_len, :] * w[0:1, :]
    for j in range(1, CONV_W):
        out = out + ext_ref[:, base + j:base + j + t_len, :] * w[j:j + 1, :]
    ext_ref[:, 0:SUBLANES, :] = ext_ref[:, t_len:t_len + SUBLANES, :]
    return out.reshape(sb * t_len, width)


def _mix_body(x_ref, h0_ref, tr0_ref, tm0_ref, c0_ref, n0_ref, m0_ref, s0_ref,
              gmix_ref, win_ref, cwr_ref, cbr_ref, wa_ref, wx_ref, ba_ref, bx_ref, lam_ref,
              cwm_ref, cbm_ref, wq_ref, wk_ref, wv_ref, wifi_ref, wiff_ref, bi_ref, bf_ref,
              mlg_ref, mls_ref, wup_ref, bup_ref, glg_ref, wred_ref, wout_ref,
              y_ref, h_ref, tro_ref, tmo_ref, c_ref, n_ref, m_ref, s_ref,
              extr, extm, z_ref, mq_ref, mk_ref, mv_ref, b_ref, cc_ref, hm_ref, bcl_ref, og_ref,
              yr_ref, cm_ref, st_ref, *, sb, t_len):
    chunk = pl.program_id(1)
    n_rows = sb * t_len

    @pl.when(chunk == 0)
    def _load_state():
        h_ref[...] = h0_ref[...]
        extr[:, 0:SUBLANES, :] = tr0_ref[...]
        extm[:, 0:SUBLANES, :] = tm0_ref[...]
        c_ref[...] = c0_ref[...]
        n_ref[...] = n0_ref[...]
        m_ref[...] = m0_ref[...]

        def load(s, carry):
            for h in range(GLA_H):
                st_ref[s, h] = s0_ref[s, h].T
            return carry

        lax.fori_loop(0, sb, load, 0)

    x = x_ref[...].reshape(n_rows, D_MODEL)
    hn = _rms(x, gmix_ref[...]).astype(bf16)
    z_ref[...] = _dot(hn, win_ref[...])

    xr = _causal_conv(extr, z_ref[:, Z_UR:Z_UR + LRU_W], cwr_ref, cbr_ref, sb, t_len)
    xr_b = xr.astype(bf16)
    r = jax.nn.sigmoid(_dot(xr_b, wa_ref[...]) + ba_ref[...])
    ig = jax.nn.sigmoid(_dot(xr_b, wx_ref[...]) + bx_ref[...])
    log_a = -LRU_C * r * jax.nn.softplus(-lam_ref[...])
    a = jnp.exp(log_a)
    bt = jnp.sqrt(1.0 - a * a) * (ig * xr)
    tl = _row_time((n_rows, LRU_W), t_len)
    s = 1
    while s < t_len:
        keep = tl >= s
        a_prev = jnp.where(keep, pltpu.roll(a, s, 0), 1.0)
        b_prev = jnp.where(keep, pltpu.roll(bt, s, 0), 0.0)
        bt = a * b_prev + bt
        a = a * a_prev
        s *= 2
    h0 = jnp.broadcast_to(h_ref[...][:, None, :], (sb, t_len, LRU_W)).reshape(n_rows, LRU_W)
    hs = bt + a * h0
    h_ref[...] = hs.reshape(sb, t_len, LRU_W)[:, t_len - 1, :]
    yr_ref[...] = jax.nn.gelu(z_ref[:, Z_GR:Z_GR + LRU_W]) * hs

    u_m = z_ref[:, Z_UM:Z_UM + ML_W]
    cm = jax.nn.silu(_causal_conv(extm, u_m, cwm_ref, cbm_ref, sb, t_len))
    cm_ref[...] = cm
    cm_b = cm.astype(bf16)
    mq = _dot(cm_b, wq_ref[...])
    mk = _dot(cm_b, wk_ref[...])
    mv = _dot(u_m.astype(bf16), wv_ref[...])
    mq_ref[...] = mq
    mk_ref[...] = mk
    mv_ref[...] = mv
    mq_b, mk_b, mv_b = mq.astype(bf16), mk.astype(bf16), mv.astype(bf16)

    def gate(w_ref, bias_ref):
        return (_dot(mq_b, w_ref[0:ML_W, :]) + _dot(mk_b, w_ref[ML_W:2 * ML_W, :])
                + _dot(mv_b, w_ref[2 * ML_W:3 * ML_W, :]) + bias_ref[...])

    li = gate(wifi_ref, bi_ref)
    bcum = _seg_cumsum(jax.nn.log_sigmoid(gate(wiff_ref, bf_ref)), t_len)
    b_ref[...] = bcum
    cc_ref[...] = li - bcum

    row_i = lax.broadcasted_iota(jnp.int32, (t_len, t_len), 0)
    col_i = lax.broadcasted_iota(jnp.int32, (t_len, t_len), 1)
    eye = row_i == col_i
    causal = row_i >= col_i

    def mlstm_seq(s, carry):
        rows = pl.ds(pl.multiple_of(s * t_len, t_len), t_len)
        m_all = m_ref[s]
        head_lane = lax.broadcasted_iota(jnp.int32, (1, ML_H), 1)
        m_out = m_all
        for h in range(ML_H):
            lanes = slice(h * ML_D, (h + 1) * ML_D)
            q = mq_ref[rows, lanes] * (ML_D ** -0.5)
            k = mk_ref[rows, lanes]
            v = mv_ref[rows, lanes]
            q_b, k_b, v_b = q.astype(bf16), k.astype(bf16), v.astype(bf16)
            bcol = b_ref[rows, h:h + 1]
            ccol = cc_ref[rows, h:h + 1]
            crow = jnp.sum(jnp.where(eye, ccol, 0.0), axis=0, keepdims=True)
            dmat = jnp.where(causal, bcol + crow, -jnp.inf)
            m_prev = m_all[:, h:h + 1]
            m_inter = bcol + m_prev
            m_t = jnp.maximum(m_inter, jnp.max(dmat, axis=-1, keepdims=True))
            w_intra = jnp.exp(dmat - m_t)
            w_inter = jnp.exp(m_inter - m_t)
            sc = _dot_nt(q_b, k_b) * w_intra
            c_old = c_ref[s, h]
            n_old = n_ref[s, h:h + 1, :]
            num = _dot(sc.astype(bf16), v_b) + _dot(q_b, c_old.astype(bf16)) * w_inter
            den = (jnp.sum(sc, axis=-1, keepdims=True)
                   + w_inter * jnp.sum(q * n_old, axis=-1, keepdims=True))
            hh = num / jnp.maximum(jnp.abs(den), jnp.exp(-m_t))
            hh = hh * lax.rsqrt(jnp.mean(hh * hh, axis=-1, keepdims=True) + EPS)
            hm_ref[rows, lanes] = hh
            m_new = m_t[t_len - 1:t_len, :]
            b_last = bcol[t_len - 1:t_len, :]
            wk = jnp.exp(b_last + ccol - m_new)
            decay = jnp.exp(b_last + m_prev - m_new)
            kw = k * wk
            c_ref[s, h] = decay * c_old + _dot_tn(kw.astype(bf16), v_b)
            n_ref[s, h:h + 1, :] = decay * n_old + jnp.sum(kw, axis=0, keepdims=True)
            m_out = jnp.where(head_lane == h, m_new, m_out)
        m_ref[s] = m_out
        return carry

    lax.fori_loop(0, sb, mlstm_seq, 0)

    sub = min(GLA_SUB, t_len)
    n_sub = t_len // sub
    al_b = z_ref[:, Z_AL:Z_AL + LANES].astype(bf16)
    lg = jax.nn.log_sigmoid(_dot(al_b, wup_ref[...]) + bup_ref[...]) * (1.0 / GLA_TAU)
    bcl = _seg_cumsum(lg, sub)
    bcl_ref[...] = bcl
    gq = z_ref[:, Z_QG:Z_QG + GLA_KP] * (GLA_DK ** -0.5)
    gk = z_ref[:, Z_KG:Z_KG + GLA_KP]
    gv = z_ref[:, Z_VG:Z_VG + GLA_VW]
    ts = _row_time((n_rows, GLA_KP), sub)
    o_intra = _dot((gq * gk).astype(bf16), wred_ref[...]) * gv
    for d in range(1, sub):
        valid = ts >= d
        diff = jnp.where(valid, bcl - pltpu.roll(bcl, d, 0), 0.0)
        e = jnp.where(valid, gq * pltpu.roll(gk, d, 0) * jnp.exp(diff), 0.0)
        o_intra = o_intra + _dot(e.astype(bf16), wred_ref[...]) * pltpu.roll(gv, d, 0)
    og_ref[...] = o_intra

    def gla_seq(s, carry):
        for j in range(n_sub):
            rows = pl.ds(pl.multiple_of(s * t_len + j * sub, sub), sub)
            bl = bcl_ref[rows, :]
            last = bl[sub - 1:sub, :]
            qt = z_ref[rows, Z_QG:Z_QG + GLA_KP] * (GLA_DK ** -0.5) * jnp.exp(bl)
            kh = z_ref[rows, Z_KG:Z_KG + GLA_KP] * jnp.exp(last - bl)
            dec = jnp.exp(last)
            v = z_ref[rows, Z_VG:Z_VG + GLA_VW]
            for h in range(GLA_H):
                kl = slice(h * GLA_DK, (h + 1) * GLA_DK)
                vl = slice(h * GLA_DV, (h + 1) * GLA_DV)
                st = st_ref[s, h]
                og_ref[rows, vl] = og_ref[rows, vl] + _dot_nt(qt[:, kl].astype(bf16), st.astype(bf16))
                st_ref[s, h] = st * dec[:, kl] + _dot_tn(v[:, vl].astype(bf16), kh[:, kl].astype(bf16))
        return carry

    lax.fori_loop(0, sb, gla_seq, 0)

    for h in range(GLA_H):
        vl = slice(h * GLA_DV, (h + 1) * GLA_DV)
        o = og_ref[:, vl]
        og_ref[:, vl] = o * lax.rsqrt(jnp.mean(o * o, axis=-1, keepdims=True) + EPS)

    y_m = jax.nn.sigmoid(z_ref[:, Z_ZM:Z_ZM + ML_W]) * (hm_ref[...] * mlg_ref[...] + mls_ref[...] * cm_ref[...])
    y_g = og_ref[...] * glg_ref[...] * jax.nn.silu(z_ref[:, Z_GG:Z_GG + GLA_VW])
    y = (_dot(yr_ref[...].astype(bf16), wout_ref[0:LRU_W, :])
         + _dot(y_m.astype(bf16), wout_ref[LRU_W:LRU_W + ML_W, :])
         + _dot(y_g.astype(bf16), wout_ref[LRU_W + ML_W:, :]))
    y_ref[...] = x_ref[...] + y.reshape(sb, t_len, D_MODEL)

    @pl.when(chunk == pl.num_programs(1) - 1)
    def _store_state():
        tro_ref[...] = extr[:, 0:SUBLANES, :]
        tmo_ref[...] = extm[:, 0:SUBLANES, :]

        def store(s, carry):
            for h in range(GLA_H):
                s_ref[s, h] = st_ref[s, h].T
            return carry

        lax.fori_loop(0, sb, store, 0)


_STATE_KEYS = ("lru_h", "lru_tail", "ml_tail", "ml_c", "ml_n", "ml_m", "gla_s")
_MIX_WEIGHT_KEYS = ("mix_g", "w_in", "lru_cw", "lru_cb", "lru_wa", "lru_wx", "lru_ba", "lru_bx", "lru_lam",
                    "ml_cw", "ml_cb", "ml_wq", "ml_wk", "ml_wv", "ml_wif_i", "ml_wif_f", "ml_b_i", "ml_b_f",
                    "ml_norm_g", "ml_skip", "gla_wup", "gla_bup", "gla_norm_g", "gla_wred", "w_out")


def _mix(x3d, state, lw, *, sb, t_len):
    n_seq, length, _ = x3d.shape
    n_rows = sb * t_len
    grid = (n_seq // sb, length // t_len)

    def seq_spec(arr):
        blk = (sb,) + arr.shape[1:]
        zeros = (0,) * (arr.ndim - 1)
        return pl.BlockSpec(blk, lambda i, c: (i,) + zeros)

    states = [state[k] for k in _STATE_KEYS]
    weights = [lw[k] for k in _MIX_WEIGHT_KEYS]
    x_spec = pl.BlockSpec((sb, t_len, D_MODEL), lambda i, c: (i, c, 0))
    outs = pl.pallas_call(
        functools.partial(_mix_body, sb=sb, t_len=t_len),
        out_shape=[jax.ShapeDtypeStruct(x3d.shape, f32)] + [jax.ShapeDtypeStruct(a.shape, f32) for a in states],
        grid=grid,
        in_specs=[x_spec] + [seq_spec(a) for a in states] + [_const_spec(w.shape) for w in weights],
        out_specs=[x_spec] + [seq_spec(a) for a in states],
        scratch_shapes=[
            pltpu.VMEM((sb, t_len + SUBLANES, LRU_W), f32),
            pltpu.VMEM((sb, t_len + SUBLANES, ML_W), f32),
            pltpu.VMEM((n_rows, Z_W), f32),
            pltpu.VMEM((n_rows, ML_W), f32),
            pltpu.VMEM((n_rows, ML_W), f32),
            pltpu.VMEM((n_rows, ML_W), f32),
            pltpu.VMEM((n_rows, LANES), f32),
            pltpu.VMEM((n_rows, LANES), f32),
            pltpu.VMEM((n_rows, ML_W), f32),
            pltpu.VMEM((n_rows, GLA_KP), f32),
            pltpu.VMEM((n_rows, GLA_VW), f32),
            pltpu.VMEM((n_rows, LRU_W), f32),
            pltpu.VMEM((n_rows, ML_W), f32),
            pltpu.VMEM((sb, GLA_H, GLA_DV, GLA_DK), f32),
        ],
        compiler_params=pltpu.CompilerParams(
            dimension_semantics=("arbitrary", "arbitrary"), vmem_limit_bytes=VMEM_LIMIT_BYTES),
        name="mixer",
    )(x3d, *states, *weights)
    return outs[0], dict(zip(_STATE_KEYS, outs[1:]))


def _block_diag(blocks):
    n, d_in, d_out = blocks.shape
    out = jnp.zeros((n * d_in, n * d_out), blocks.dtype)
    for i in range(n):
        out = out.at[i * d_in:(i + 1) * d_in, i * d_out:(i + 1) * d_out].set(blocks[i])
    return out


def _pad_cols(w, width):
    return jnp.pad(w, ((0, 0), (0, width - w.shape[1])))


def _ffn_weights(w1, w3, w2):
    w1t = w1.reshape(D_MODEL, N_FF_TILES, FF_TILE).transpose(1, 0, 2)
    w3t = w3.reshape(D_MODEL, N_FF_TILES, FF_TILE).transpose(1, 0, 2)
    w13 = jnp.concatenate([w1t, w3t], axis=-1).astype(bf16)
    return w13, w2.reshape(N_FF_TILES, FF_TILE, D_MODEL).astype(bf16)


def _row(v, width=None):
    v = v.reshape(1, -1).astype(f32)
    return v if width is None else _pad_cols(v, width)


def _tail(conv_state):
    return jnp.pad(conv_state.astype(f32), ((0, 0), (SUBLANES - (CONV_W - 1), 0), (0, 0)))


def _untail(tail):
    return tail[:, SUBLANES - (CONV_W - 1):, :]


def kernel(x_prompt, x_sample, state_lru_h, state_lru_conv, state_mlstm_C, state_mlstm_n, state_mlstm_m, state_mlstm_conv, state_gla_S, meta_tokens, ffn1_norm_g, ffn1_w1, ffn1_w3, ffn1_w2, mix_norm_g, w_in, lru_conv_w, lru_conv_b, lru_wa, lru_ba, lru_wx, lru_bx, lru_lambda, ml_conv_w, ml_conv_b, ml_wq, ml_wk, ml_wv, ml_w_if, ml_b_if, ml_norm_g, ml_skip, gla_w_up, gla_b_up, gla_norm_g, w_out, ffn2_norm_g, ffn2_w1, ffn2_w3, ffn2_w2, final_norm_g):
    n_prompt = x_prompt.shape[0]
    n_sample, t_sample, _ = x_sample.shape

    head_of_k = jnp.arange(GLA_KP) // GLA_DK
    head_of_v = jnp.arange(GLA_VW) // GLA_DV
    wred = (head_of_k[:, None] == head_of_v[None, :]).astype(bf16)

    layers = []
    for l in range(DEPTH):
        w = w_in[l]
        sizes = (LRU_W, LRU_W, ML_W, ML_W, GLA_KW, GLA_KW, GLA_VW, GLA_VW, GLA_RANK)
        widths = (LRU_W, LRU_W, ML_W, ML_W, GLA_KP, GLA_KP, GLA_VW, GLA_VW, LANES)
        parts, off = [], 0
        for size, width in zip(sizes, widths):
            parts.append(_pad_cols(w[:, off:off + size], width))
            off += size
        f1_w13, f1_w2 = _ffn_weights(ffn1_w1[l], ffn1_w3[l], ffn1_w2[l])
        f2_w13, f2_w2 = _ffn_weights(ffn2_w1[l], ffn2_w3[l], ffn2_w2[l])
        wif = ml_w_if[l]
        layers.append(dict(
            f1_g=_row(ffn1_norm_g[l]), f1_w13=f1_w13, f1_w2=f1_w2,
            f2_g=_row(ffn2_norm_g[l]), f2_w13=f2_w13, f2_w2=f2_w2,
            mix_g=_row(mix_norm_g[l]),
            w_in=jnp.concatenate(parts, axis=1).astype(bf16),
            lru_cw=lru_conv_w[l].astype(f32), lru_cb=_row(lru_conv_b[l]),
            lru_wa=_block_diag(lru_wa[l]).astype(bf16), lru_wx=_block_diag(lru_wx[l]).astype(bf16),
            lru_ba=_row(lru_ba[l]), lru_bx=_row(lru_bx[l]), lru_lam=_row(lru_lambda[l]),
            ml_cw=ml_conv_w[l].astype(f32), ml_cb=_row(ml_conv_b[l]),
            ml_wq=_block_diag(ml_wq[l]).astype(bf16), ml_wk=_block_diag(ml_wk[l]).astype(bf16),
            ml_wv=_block_diag(ml_wv[l]).astype(bf16),
            ml_wif_i=_pad_cols(wif[:, :ML_H], LANES).astype(bf16),
            ml_wif_f=_pad_cols(wif[:, ML_H:], LANES).astype(bf16),
            ml_b_i=_row(ml_b_if[l][:ML_H], LANES), ml_b_f=_row(ml_b_if[l][ML_H:], LANES),
            ml_norm_g=_row(ml_norm_g[l]), ml_skip=_row(ml_skip[l]),
            gla_wup=jnp.pad(gla_w_up[l], ((0, LANES - GLA_RANK), (0, GLA_KP - GLA_KW))).astype(bf16),
            gla_bup=_row(gla_b_up[l], GLA_KP), gla_norm_g=_row(gla_norm_g[l]),
            gla_wred=wred, w_out=w_out[l].astype(bf16),
        ))
    final_g = _row(final_norm_g)

    def run_trunk(x3d, states, *, sb, t_len, tm, need_output):
        n_seq, length, _ = x3d.shape
        new_states = []
        for l in range(DEPTH):
            lw = layers[l]
            x2d = _ffn(x3d.reshape(n_seq * length, D_MODEL), lw["f1_g"], lw["f1_w13"], lw["f1_w2"], final_g,
                       tm=tm, post_norm=False)
            x3d, st = _mix(x2d.reshape(n_seq, length, D_MODEL), states[l], lw, sb=sb, t_len=t_len)
            new_states.append(st)
            last = l == DEPTH - 1
            if need_output or not last:
                x2d = _ffn(x3d.reshape(n_seq * length, D_MODEL), lw["f2_g"], lw["f2_w13"], lw["f2_w2"], final_g,
                           tm=tm, post_norm=last)
                x3d = x2d.reshape(n_seq, length, D_MODEL)
        return x3d, new_states

    zero_state = dict(
        lru_h=jnp.zeros((n_prompt, LRU_W), f32), lru_tail=jnp.zeros((n_prompt, SUBLANES, LRU_W), f32),
        ml_tail=jnp.zeros((n_prompt, SUBLANES, ML_W), f32), ml_c=jnp.zeros((n_prompt, ML_H, ML_D, ML_D), f32),
        ml_n=jnp.zeros((n_prompt, ML_H, ML_D), f32), ml_m=jnp.zeros((n_prompt, 1, ML_H), f32),
        gla_s=jnp.zeros((n_prompt, GLA_H, GLA_DK, GLA_DV), f32))
    x_meta = jnp.broadcast_to(meta_tokens.astype(f32)[None], (n_prompt, N_META, D_MODEL))
    _, meta_states = run_trunk(x_meta, [zero_state] * DEPTH, sb=n_prompt, t_len=N_META,
                               tm=n_prompt * N_META, need_output=False)

    y_prompt, p_states = run_trunk(x_prompt, meta_states, sb=n_prompt, t_len=CHUNK, tm=512, need_output=True)

    s_states = [dict(lru_h=state_lru_h[l].astype(f32), lru_tail=_tail(state_lru_conv[l]),
                     ml_tail=_tail(state_mlstm_conv[l]), ml_c=state_mlstm_C[l].astype(f32),
                     ml_n=state_mlstm_n[l].astype(f32), ml_m=state_mlstm_m[l].astype(f32)[:, None, :],
                     gla_s=state_gla_S[l].astype(f32)) for l in range(DEPTH)]
    y_sample, s_states = run_trunk(x_sample, s_states, sb=16, t_len=t_sample, tm=512, need_output=True)

    def stacked(states):
        st = lambda k: jnp.stack([s[k] for s in states])
        return (st("lru_h"), jnp.stack([_untail(s["lru_tail"]) for s in states]), st("ml_c"), st("ml_n"),
                st("ml_m")[:, :, 0, :], jnp.stack([_untail(s["ml_tail"]) for s in states]), st("gla_s"))

    return (y_prompt, y_sample) + stacked(p_states) + stacked(s_states)
```

```python
import functools

import jax
import jax.numpy as jnp
from jax import lax
from jax.experimental import pallas as pl
from jax.experimental.pallas import tpu as pltpu

f32 = jnp.float32
bf16 = jnp.bfloat16

D_MODEL = 1024
DEPTH = 2
N_META = 16
CONV_W = 4
CHUNK = 64
LRU_W = 256
LRU_BLOCKS = 4
LRU_C = 8.0
ML_H = 4
ML_D = 96
ML_W = ML_H * ML_D
GLA_H = 4
GLA_DK = 48
GLA_DV = 96
GLA_KW = GLA_H * GLA_DK
GLA_VW = GLA_H * GLA_DV
GLA_RANK = 16
GLA_TAU = 16.0
D_FF = 2816
EPS = 1e-6

SUBLANES = 8
LANES = 128
VMEM_LIMIT_BYTES = 60 * 1024 * 1024
BF16_SUBLANES = 16

FF_TILE = 256
N_FF_TILES = D_FF // FF_TILE
GLA_SUB = 16

Z_UR = 0
Z_GR = Z_UR + LRU_W
Z_UM = Z_GR + LRU_W
Z_ZM = Z_UM + ML_W
Z_QG = Z_ZM + ML_W
GLA_KP = 256
Z_KG = Z_QG + GLA_KP
Z_VG = Z_KG + GLA_KP
Z_GG = Z_VG + GLA_VW
Z_AL = Z_GG + GLA_VW
Z_W = Z_AL + LANES

ML_CW = ML_W + LANES


def _dot(a, b):
    return jnp.dot(a, b, preferred_element_type=f32)


def _dot_nt(a, b):
    return lax.dot_general(a, b, (((1,), (1,)), ((), ())), preferred_element_type=f32)


def _dot_tn(a, b):
    return lax.dot_general(a, b, (((0,), (0,)), ((), ())), preferred_element_type=f32)


def _rms(x, g):
    return x * lax.rsqrt(jnp.mean(x * x, axis=-1, keepdims=True) + EPS) * g


def _ffn_body(x_ref, g_ref, w1_ref, w3_ref, w2_ref, gf_ref, o_ref, gated_ref, *, post_norm):
    x = x_ref[...]
    h = _rms(x, g_ref[...]).astype(bf16)
    for f in range(N_FF_TILES):
        cols = slice(f * FF_TILE, (f + 1) * FF_TILE)
        a = _dot(h, w1_ref[:, cols])
        b = _dot(h, w3_ref[:, cols])
        gated_ref[:, cols] = (a * jax.nn.sigmoid(a) * b).astype(bf16)
    y = x + 0.5 * _dot(gated_ref[...], w2_ref[...])
    if post_norm:
        y = _rms(y, gf_ref[...])
    o_ref[...] = y


def _const_spec(shape):
    zeros = (0,) * len(shape)
    return pl.BlockSpec(shape, lambda *_: zeros, pipeline_mode=pl.Buffered(1))


def _ffn(x2d, g, w1, w3, w2, gf, *, tm, post_norm):
    rows = x2d.shape[0]
    return pl.pallas_call(
        functools.partial(_ffn_body, post_norm=post_norm),
        out_shape=jax.ShapeDtypeStruct(x2d.shape, f32),
        grid=(rows // tm,),
        in_specs=[
            pl.BlockSpec((tm, D_MODEL), lambda i: (i, 0)),
            _const_spec(g.shape),
            _const_spec(w1.shape),
            _const_spec(w3.shape),
            _const_spec(w2.shape),
            _const_spec(gf.shape),
        ],
        out_specs=pl.BlockSpec((tm, D_MODEL), lambda i: (i, 0)),
        scratch_shapes=[pltpu.VMEM((tm, D_FF), bf16)],
        compiler_params=pltpu.CompilerParams(
            dimension_semantics=("arbitrary",), vmem_limit_bytes=VMEM_LIMIT_BYTES),
        name="ffn",
    )(x2d, g, w1, w3, w2, gf)


def _row_time(shape, period):
    return lax.broadcasted_iota(jnp.int32, shape, 0) % period


def _seg_cumsum(x, period):
    t = _row_time(x.shape, period)
    s = 1
    while s < period:
        x = x + jnp.where(t >= s, pltpu.roll(x, s, 0), 0.0)
        s *= 2
    return x


def _group_of(idx, width, groups):
    g = jnp.zeros_like(idx)
    for i in range(1, groups):
        g = g + (idx >= i * width).astype(jnp.int32)
    return g


def _head_mask(shape, row_width, lane_width, extra_lane0=None):
    r = lax.broadcasted_iota(jnp.int32, shape, 0)
    l = lax.broadcasted_iota(jnp.int32, shape, 1)
    hr = _group_of(r, row_width, ML_H)
    n_lane_groups = -(-shape[1] // lane_width)
    hl = _group_of(l, lane_width, n_lane_groups)
    if extra_lane0 is not None:
        hl = jnp.where(l >= extra_lane0, l - extra_lane0, hl)
    return hr == hl


def _rows_from_seq(v, t_len):
    sb, width = v.shape
    return jnp.broadcast_to(v[:, None, :], (sb, t_len, width)).reshape(sb * t_len, width)


def _last_rows(x, groups, period):
    return x.reshape(groups, period, x.shape[-1])[:, period - 1, :]


def _expand(x, e):
    hi = x.astype(bf16)
    lo = (x - hi.astype(f32)).astype(bf16)
    return _dot(hi, e) + _dot(lo, e)


def _causal_conv(ext_ref, u, w_ref, b_ref, sb, t_len):
    width = u.shape[-1]
    ext_ref[:, SUBLANES:SUBLANES + t_len, :] = u.reshape(sb, t_len, width)
    w = w_ref[...]
    base = SUBLANES - (CONV_W - 1)
    out = b_ref[...] + ext_ref[:, base:base + t_len, :] * w[0:1, :]
    for j in range(1, CONV_W):
        out = out + ext_ref[:, base + j:base + j + t_len, :] * w[j:j + 1, :]
    ext_ref[:, 0:SUBLANES, :] = ext_ref[:, t_len:t_len + SUBLANES, :]
    return out.reshape(sb * t_len, width)


def _mix_body(x_ref, h0_ref, tr0_ref, tm0_ref, c0_ref, n0_ref, m0_ref, s0_ref,
              gmix_ref, win_ref, cwr_ref, cbr_ref, wa_ref, wx_ref, ba_ref, bx_ref, lam_ref,
              cwm_ref, cbm_ref, wq_ref, wk_ref, wv_ref, wifi_ref, wiff_ref, bi_ref, bf_ref,
              mlg_ref, mls_ref, wup_ref, bup_ref, glg_ref, wred_ref, wout_ref,
              ert_ref, ehd_ref, ones_ref, wms_ref,
              y_ref, h_ref, tro_ref, tmo_ref, c_ref, n_ref, m_ref, s_ref,
              extr, extm, z_ref, mq_ref, mk_ref, kw_ref, va_ref, rtx_ref, wix_ref, eix_ref, cc_ref,
              decx_ref, hm_ref, qt_ref, kh_ref, gdec_ref, og_ref, yr_ref, cm_ref, cbd_ref, stbd_ref,
              *, sb, t_len):
    chunk = pl.program_id(1)
    n_rows = sb * t_len
    eye_d = (lax.broadcasted_iota(jnp.int32, (ML_D, ML_D), 0)
             == lax.broadcasted_iota(jnp.int32, (ML_D, ML_D), 1))

    @pl.when(chunk == 0)
    def _load_state():
        h_ref[...] = h0_ref[...]
        extr[:, 0:SUBLANES, :] = tr0_ref[...]
        extm[:, 0:SUBLANES, :] = tm0_ref[...]
        m_ref[...] = m0_ref[...]

        def load(s, carry):
            cbd_ref[s] = jnp.zeros((ML_W, ML_CW), f32)
            stbd_ref[s] = jnp.zeros((GLA_VW, GLA_KP), f32)
            for h in range(ML_H):
                rows = slice(h * ML_D, (h + 1) * ML_D)
                cbd_ref[s, rows, rows] = c0_ref[s, h]
                n_col = jnp.sum(jnp.where(eye_d, n0_ref[s, h:h + 1, :], 0.0), axis=1, keepdims=True)
                cbd_ref[s, rows, ML_W + h:ML_W + h + 1] = n_col
                stbd_ref[s, h * GLA_DV:(h + 1) * GLA_DV, h * GLA_DK:(h + 1) * GLA_DK] = s0_ref[s, h].T
            return carry

        lax.fori_loop(0, sb, load, 0)

    x = x_ref[...].reshape(n_rows, D_MODEL)
    hn = _rms(x, gmix_ref[...]).astype(bf16)
    z_ref[...] = _dot(hn, win_ref[...])

    xr = _causal_conv(extr, z_ref[:, Z_UR:Z_UR + LRU_W], cwr_ref, cbr_ref, sb, t_len)
    xr_b = xr.astype(bf16)
    r = jax.nn.sigmoid(_dot(xr_b, wa_ref[...]) + ba_ref[...])
    ig = jax.nn.sigmoid(_dot(xr_b, wx_ref[...]) + bx_ref[...])
    log_a = -LRU_C * r * jax.nn.softplus(-lam_ref[...])
    a = jnp.exp(log_a)
    bt = jnp.sqrt(1.0 - a * a) * (ig * xr)
    tl = _row_time((n_rows, LRU_W), t_len)
    s = 1
    while s < t_len:
        keep = tl >= s
        a_prev = jnp.where(keep, pltpu.roll(a, s, 0), 1.0)
        b_prev = jnp.where(keep, pltpu.roll(bt, s, 0), 0.0)
        bt = a * b_prev + bt
        a = a * a_prev
        s *= 2
    hs = bt + a * _rows_from_seq(h_ref[...], t_len)
    h_ref[...] = _last_rows(hs, sb, t_len)
    yr_ref[...] = jax.nn.gelu(z_ref[:, Z_GR:Z_GR + LRU_W]) * hs

    u_m = z_ref[:, Z_UM:Z_UM + ML_W]
    cm = jax.nn.silu(_causal_conv(extm, u_m, cwm_ref, cbm_ref, sb, t_len))
    cm_ref[...] = cm
    cm_b = cm.astype(bf16)
    mq = _dot(cm_b, wq_ref[...])
    mk = _dot(cm_b, wk_ref[...])
    va = _dot(u_m.astype(bf16), wv_ref[...]) + ones_ref[...]
    mq_b, mk_b, va_b = mq.astype(bf16), mk.astype(bf16), va.astype(bf16)

    def gate(w_ref, bias_ref):
        return (_dot(mq_b, w_ref[0:ML_W, :]) + _dot(mk_b, w_ref[ML_W:2 * ML_W, :])
                + _dot(va_b, w_ref[2 * ML_W:2 * ML_W + ML_CW, :]) + bias_ref[...])

    li = gate(wifi_ref, bi_ref)
    lf = jax.nn.log_sigmoid(gate(wiff_ref, bf_ref))
    tg = _row_time((n_rows, LANES), t_len)
    bcum, m_loc = lf, li
    s = 1
    while s < t_len:
        keep = tg >= s
        b_prev = jnp.where(keep, pltpu.roll(bcum, s, 0), 0.0)
        m_prev_seg = jnp.where(keep, pltpu.roll(m_loc, s, 0), -jnp.inf)
        m_loc = jnp.maximum(m_prev_seg + bcum, m_loc)
        bcum = bcum + b_prev
        s *= 2
    m_old = m_ref[:, 0, :]
    m_old_rows = _rows_from_seq(m_old, t_len)
    m_t = jnp.maximum(bcum + m_old_rows, m_loc)
    m_new = _last_rows(m_t, sb, t_len)
    b_last = _last_rows(bcum, sb, t_len)
    m_ref[...] = m_new[:, None, :]
    cc = li - bcum
    cc_ref[...] = cc
    e_hd = ehd_ref[...]
    e_h = ehd_ref[:, 0:ML_W]
    rtx_ref[...] = _expand(bcum - m_t, ert_ref[...])
    wix_ref[...] = jnp.exp(_expand(bcum + m_old_rows - m_t, e_hd))
    eix_ref[...] = jnp.exp(_expand(-m_t, e_h))
    w_k = jnp.exp(_expand(_rows_from_seq(b_last, t_len) + cc - _rows_from_seq(m_new, t_len), e_h))
    decx_ref[...] = jnp.exp(_expand(b_last + m_old - m_new, e_hd))[:, None, :]
    mq_ref[...] = (mq * (ML_D ** -0.5)).astype(mq_ref.dtype)
    mk_ref[...] = mk
    kw_ref[...] = (mk * w_k).astype(kw_ref.dtype)
    va_ref[...] = va

    row_t = lax.broadcasted_iota(jnp.int32, (t_len, ML_H * t_len), 0)
    lane_s = lax.broadcasted_iota(jnp.int32, (t_len, ML_H * t_len), 1) % t_len
    causal_cat = row_t >= lane_s
    mask_k = _head_mask((ML_H * t_len, ML_W), t_len, ML_D)
    mask_v = _head_mask((ML_H * t_len, ML_CW), t_len, ML_D, extra_lane0=ML_W)
    mask_c = _head_mask((ML_W, ML_CW), ML_D, ML_D, extra_lane0=ML_W)
    head_lane = _group_of(lax.broadcasted_iota(jnp.int32, (t_len, ML_W), 1), ML_D, ML_H)

    def mlstm_seq(s, carry):
        rows = pl.ds(pl.multiple_of(s * t_len, t_len), t_len)
        q_b = mq_ref[rows, :].astype(bf16)
        k = mk_ref[rows, :]
        va_s = va_ref[rows, :]
        va_sb = va_s.astype(bf16)
        c_t = cc_ref[rows, :].T
        c_row = jnp.concatenate([c_t[h:h + 1, :] for h in range(ML_H)], axis=1)
        dmat = jnp.where(causal_cat, rtx_ref[rows, :] + c_row, -jnp.inf)
        k_bd = jnp.where(mask_k, jnp.concatenate([k] * ML_H, axis=0), 0.0).astype(bf16)
        v_bd = jnp.where(mask_v, jnp.concatenate([va_s] * ML_H, axis=0), 0.0).astype(bf16)
        sc = _dot_nt(q_b, k_bd) * jnp.exp(dmat)
        c_old = cbd_ref[s]
        num = _dot(sc.astype(bf16), v_bd) + _dot(q_b, c_old.astype(bf16)) * wix_ref[rows, :]
        den = num[:, ML_W:]
        den_x = jnp.zeros((t_len, ML_W), f32)
        for h in range(ML_H):
            den_x = jnp.where(head_lane == h, den[:, h:h + 1], den_x)
        hm_ref[rows, :] = num[:, 0:ML_W] / jnp.maximum(jnp.abs(den_x), eix_ref[rows, :])
        upd = _dot_tn(kw_ref[rows, :].astype(bf16), va_sb)
        cbd_ref[s] = c_old * decx_ref[s] + jnp.where(mask_c, upd, 0.0)
        return carry

    lax.fori_loop(0, sb, mlstm_seq, 0)

    sub = min(GLA_SUB, t_len)
    n_sub = t_len // sub
    al_b = z_ref[:, Z_AL:Z_AL + LANES].astype(bf16)
    lg = jax.nn.log_sigmoid(_dot(al_b, wup_ref[...]) + bup_ref[...]) * (1.0 / GLA_TAU)
    bcl = _seg_cumsum(lg, sub)
    gq = z_ref[:, Z_QG:Z_QG + GLA_KP] * (GLA_DK ** -0.5)
    gk = z_ref[:, Z_KG:Z_KG + GLA_KP]
    gv = z_ref[:, Z_VG:Z_VG + GLA_VW]
    last = _last_rows(bcl, n_rows // sub, sub)
    gdec_ref[...] = jnp.exp(last)[:, None, :]
    qt_ref[...] = gq * jnp.exp(bcl)
    kh_ref[...] = gk * jnp.exp(_rows_from_seq(last, sub) - bcl)
    ts = _row_time((n_rows, GLA_KP), sub)
    o_intra = _dot((gq * gk).astype(bf16), wred_ref[...]) * gv
    for d in range(1, sub):
        e = jnp.where(ts >= d, gq * pltpu.roll(gk, d, 0) * jnp.exp(bcl - pltpu.roll(bcl, d, 0)), 0.0)
        o_intra = o_intra + _dot(e.astype(bf16), wred_ref[...]) * pltpu.roll(gv, d, 0)
    og_ref[...] = o_intra

    mask_s = _head_mask((GLA_VW, GLA_KP), GLA_DV, GLA_DK)

    def gla_seq(s, carry):
        st = stbd_ref[s]
        for j in range(n_sub):
            rows = pl.ds(pl.multiple_of(s * t_len + j * sub, sub), sub)
            qt_b = qt_ref[rows, :].astype(bf16)
            kh_b = kh_ref[rows, :].astype(bf16)
            v_b = z_ref[rows, Z_VG:Z_VG + GLA_VW].astype(bf16)
            og_ref[rows, :] = og_ref[rows, :] + _dot_nt(qt_b, st.astype(bf16))
            st = st * gdec_ref[s * n_sub + j] + jnp.where(mask_s, _dot_tn(v_b, kh_b), 0.0)
        stbd_ref[s] = st
        return carry

    lax.fori_loop(0, sb, gla_seq, 0)

    def head_norm(o):
        ms = _dot((o * o).astype(bf16), wms_ref[...]) * (1.0 / ML_D)
        return o * lax.rsqrt(ms + EPS)

    y_m = (jax.nn.sigmoid(z_ref[:, Z_ZM:Z_ZM + ML_W])
           * (head_norm(hm_ref[...]) * mlg_ref[...] + mls_ref[...] * cm_ref[...]))
    y_g = head_norm(og_ref[...]) * glg_ref[...] * jax.nn.silu(z_ref[:, Z_GG:Z_GG + GLA_VW])
    y = (_dot(yr_ref[...].astype(bf16), wout_ref[0:LRU_W, :])
         + _dot(y_m.astype(bf16), wout_ref[LRU_W:LRU_W + ML_W, :])
         + _dot(y_g.astype(bf16), wout_ref[LRU_W + ML_W:, :]))
    y_ref[...] = x_ref[...] + y.reshape(sb, t_len, D_MODEL)

    @pl.when(chunk == pl.num_programs(1) - 1)
    def _store_state():
        tro_ref[...] = extr[:, 0:SUBLANES, :]
        tmo_ref[...] = extm[:, 0:SUBLANES, :]

        def store(s, carry):
            for h in range(ML_H):
                rows = slice(h * ML_D, (h + 1) * ML_D)
                c_ref[s, h] = cbd_ref[s, rows, rows]
                n_col = cbd_ref[s, rows, ML_W + h:ML_W + h + 1]
                n_ref[s, h:h + 1, :] = jnp.sum(jnp.where(eye_d, n_col, 0.0), axis=0, keepdims=True)
                s_ref[s, h] = stbd_ref[s, h * GLA_DV:(h + 1) * GLA_DV, h * GLA_DK:(h + 1) * GLA_DK].T
            return carry

        lax.fori_loop(0, sb, store, 0)


_STATE_KEYS = ("lru_h", "lru_tail", "ml_tail", "ml_c", "ml_n", "ml_m", "gla_s")
_MIX_WEIGHT_KEYS = ("mix_g", "w_in", "lru_cw", "lru_cb", "lru_wa", "lru_wx", "lru_ba", "lru_bx", "lru_lam",
                    "ml_cw", "ml_cb", "ml_wq", "ml_wk", "ml_wv", "ml_wif_i", "ml_wif_f", "ml_b_i", "ml_b_f",
                    "ml_norm_g", "ml_skip", "gla_wup", "gla_bup", "gla_norm_g", "gla_wred", "w_out")


def _mix(x3d, state, lw, consts, *, sb, t_len):
    n_seq, length, _ = x3d.shape
    n_rows = sb * t_len
    n_sub = t_len // min(GLA_SUB, t_len)
    grid = (n_seq // sb, length // t_len)

    def seq_spec(arr):
        blk = (sb,) + arr.shape[1:]
        zeros = (0,) * (arr.ndim - 1)
        return pl.BlockSpec(blk, lambda i, c: (i,) + zeros, pipeline_mode=pl.Buffered(1))

    states = [state[k] for k in _STATE_KEYS]
    weights = [lw[k] for k in _MIX_WEIGHT_KEYS] + list(consts)
    x_spec = pl.BlockSpec((sb, t_len, D_MODEL), lambda i, c: (i, c, 0))
    row_scratch = lambda width, dtype=f32: pltpu.VMEM((n_rows, width), dtype)
    opnd = bf16 if t_len % BF16_SUBLANES == 0 else f32
    outs = pl.pallas_call(
        functools.partial(_mix_body, sb=sb, t_len=t_len),
        out_shape=[jax.ShapeDtypeStruct(x3d.shape, f32)] + [jax.ShapeDtypeStruct(a.shape, f32) for a in states],
        grid=grid,
        in_specs=[x_spec] + [seq_spec(a) for a in states] + [_const_spec(w.shape) for w in weights],
        out_specs=[x_spec] + [seq_spec(a) for a in states],
        scratch_shapes=[
            pltpu.VMEM((sb, t_len + SUBLANES, LRU_W), f32),
            pltpu.VMEM((sb, t_len + SUBLANES, ML_W), f32),
            row_scratch(Z_W),
            row_scratch(ML_W, opnd),
            row_scratch(ML_W),
            row_scratch(ML_W, opnd),
            row_scratch(ML_CW),
            row_scratch(ML_H * t_len),
            row_scratch(ML_CW),
            row_scratch(ML_W),
            row_scratch(LANES),
            pltpu.VMEM((sb, 1, ML_CW), f32),
            row_scratch(ML_W),
            row_scratch(GLA_KP),
            row_scratch(GLA_KP),
            pltpu.VMEM((sb * n_sub, 1, GLA_KP), f32),
            row_scratch(GLA_VW),
            row_scratch(LRU_W),
            row_scratch(ML_W),
            pltpu.VMEM((sb, ML_W, ML_CW), f32),
            pltpu.VMEM((sb, GLA_VW, GLA_KP), f32),
        ],
        compiler_params=pltpu.CompilerParams(
            dimension_semantics=("arbitrary", "arbitrary"), vmem_limit_bytes=VMEM_LIMIT_BYTES),
        name="mixer",
    )(x3d, *states, *weights)
    return outs[0], dict(zip(_STATE_KEYS, outs[1:]))


def _block_diag(blocks):
    n, d_in, d_out = blocks.shape
    out = jnp.zeros((n * d_in, n * d_out), blocks.dtype)
    for i in range(n):
        out = out.at[i * d_in:(i + 1) * d_in, i * d_out:(i + 1) * d_out].set(blocks[i])
    return out


def _pad_cols(w, width):
    return jnp.pad(w, ((0, 0), (0, width - w.shape[1])))


def _pad_rows(w, height):
    return jnp.pad(w, ((0, height - w.shape[0]), (0, 0)))


def _row(v, width=None):
    v = v.reshape(1, -1).astype(f32)
    return v if width is None else _pad_cols(v, width)


def _tail(conv_state):
    return jnp.pad(conv_state.astype(f32), ((0, 0), (SUBLANES - (CONV_W - 1), 0), (0, 0)))


def _untail(tail):
    return tail[:, SUBLANES - (CONV_W - 1):, :]


def _mix_consts(t_len):
    head = jnp.arange(LANES)[:, None]
    e_rt = (head == (jnp.arange(ML_H * t_len) // t_len)[None, :]).astype(bf16)
    lane = jnp.arange(ML_CW)
    head_of_lane = jnp.where(lane < ML_W, lane // ML_D, lane - ML_W)
    e_hd = (head == head_of_lane[None, :]).astype(bf16)
    ones_row = ((lane >= ML_W) & (lane < ML_W + ML_H)).astype(f32)[None, :]
    vh = jnp.arange(ML_W) // ML_D
    w_ms = (vh[:, None] == vh[None, :]).astype(bf16)
    return e_rt, e_hd, ones_row, w_ms


def kernel(x_prompt, x_sample, state_lru_h, state_lru_conv, state_mlstm_C, state_mlstm_n, state_mlstm_m, state_mlstm_conv, state_gla_S, meta_tokens, ffn1_norm_g, ffn1_w1, ffn1_w3, ffn1_w2, mix_norm_g, w_in, lru_conv_w, lru_conv_b, lru_wa, lru_ba, lru_wx, lru_bx, lru_lambda, ml_conv_w, ml_conv_b, ml_wq, ml_wk, ml_wv, ml_w_if, ml_b_if, ml_norm_g, ml_skip, gla_w_up, gla_b_up, gla_norm_g, w_out, ffn2_norm_g, ffn2_w1, ffn2_w3, ffn2_w2, final_norm_g):
    n_prompt = x_prompt.shape[0]
    n_sample, t_sample, _ = x_sample.shape

    head_of_k = jnp.arange(GLA_KP) // GLA_DK
    head_of_v = jnp.arange(GLA_VW) // GLA_DV
    wred = (head_of_k[:, None] == head_of_v[None, :]).astype(bf16)

    layers = []
    for l in range(DEPTH):
        w = w_in[l]
        sizes = (LRU_W, LRU_W, ML_W, ML_W, GLA_KW, GLA_KW, GLA_VW, GLA_VW, GLA_RANK)
        widths = (LRU_W, LRU_W, ML_W, ML_W, GLA_KP, GLA_KP, GLA_VW, GLA_VW, LANES)
        parts, off = [], 0
        for size, width in zip(sizes, widths):
            parts.append(_pad_cols(w[:, off:off + size], width))
            off += size
        wif = _pad_rows(ml_w_if[l], 2 * ML_W + ML_CW)
        layers.append(dict(
            f1_g=_row(ffn1_norm_g[l]), f1_w1=ffn1_w1[l].astype(bf16), f1_w3=ffn1_w3[l].astype(bf16),
            f1_w2=ffn1_w2[l].astype(bf16),
            f2_g=_row(ffn2_norm_g[l]), f2_w1=ffn2_w1[l].astype(bf16), f2_w3=ffn2_w3[l].astype(bf16),
            f2_w2=ffn2_w2[l].astype(bf16),
            mix_g=_row(mix_norm_g[l]),
            w_in=jnp.concatenate(parts, axis=1).astype(bf16),
            lru_cw=lru_conv_w[l].astype(f32), lru_cb=_row(lru_conv_b[l]),
            lru_wa=_block_diag(lru_wa[l]).astype(bf16), lru_wx=_block_diag(lru_wx[l]).astype(bf16),
            lru_ba=_row(lru_ba[l]), lru_bx=_row(lru_bx[l]), lru_lam=_row(lru_lambda[l]),
            ml_cw=ml_conv_w[l].astype(f32), ml_cb=_row(ml_conv_b[l]),
            ml_wq=_block_diag(ml_wq[l]).astype(bf16), ml_wk=_block_diag(ml_wk[l]).astype(bf16),
            ml_wv=_pad_cols(_block_diag(ml_wv[l]), ML_CW).astype(bf16),
            ml_wif_i=_pad_cols(wif[:, :ML_H], LANES).astype(bf16),
            ml_wif_f=_pad_cols(wif[:, ML_H:], LANES).astype(bf16),
            ml_b_i=_row(ml_b_if[l][:ML_H], LANES), ml_b_f=_row(ml_b_if[l][ML_H:], LANES),
            ml_norm_g=_row(ml_norm_g[l]), ml_skip=_row(ml_skip[l]),
            gla_wup=jnp.pad(gla_w_up[l], ((0, LANES - GLA_RANK), (0, GLA_KP - GLA_KW))).astype(bf16),
            gla_bup=_row(gla_b_up[l], GLA_KP), gla_norm_g=_row(gla_norm_g[l]),
            gla_wred=wred, w_out=w_out[l].astype(bf16),
        ))
    final_g = _row(final_norm_g)

    def run_trunk(x3d, states, *, sb, t_len, tm, need_output):
        n_seq, length, _ = x3d.shape
        consts = _mix_consts(t_len)
        new_states = []
        for l in range(DEPTH):
            lw = layers[l]
            x2d = _ffn(x3d.reshape(n_seq * length, D_MODEL), lw["f1_g"], lw["f1_w1"], lw["f1_w3"], lw["f1_w2"], final_g,
                       tm=tm, post_norm=False)
            x3d, st = _mix(x2d.reshape(n_seq, length, D_MODEL), states[l], lw, consts, sb=sb, t_len=t_len)
            new_states.append(st)
            last = l == DEPTH - 1
            if need_output or not last:
                x2d = _ffn(x3d.reshape(n_seq * length, D_MODEL), lw["f2_g"], lw["f2_w1"], lw["f2_w3"], lw["f2_w2"], final_g,
                           tm=tm, post_norm=last)
                x3d = x2d.reshape(n_seq, length, D_MODEL)
        return x3d, new_states

    zero_state = dict(
        lru_h=jnp.zeros((n_prompt, LRU_W), f32), lru_tail=jnp.zeros((n_prompt, SUBLANES, LRU_W), f32),
        ml_tail=jnp.zeros((n_prompt, SUBLANES, ML_W), f32), ml_c=jnp.zeros((n_prompt, ML_H, ML_D, ML_D), f32),
        ml_n=jnp.zeros((n_prompt, ML_H, ML_D), f32), ml_m=jnp.zeros((n_prompt, 1, LANES), f32),
        gla_s=jnp.zeros((n_prompt, GLA_H, GLA_DK, GLA_DV), f32))
    x_meta = jnp.broadcast_to(meta_tokens.astype(f32)[None], (n_prompt, N_META, D_MODEL))
    _, meta_states = run_trunk(x_meta, [zero_state] * DEPTH, sb=n_prompt, t_len=N_META,
                               tm=n_prompt * N_META, need_output=False)

    y_prompt, p_states = run_trunk(x_prompt, meta_states, sb=n_prompt, t_len=CHUNK, tm=512, need_output=True)

    s_states = [dict(lru_h=state_lru_h[l].astype(f32), lru_tail=_tail(state_lru_conv[l]),
                     ml_tail=_tail(state_mlstm_conv[l]), ml_c=state_mlstm_C[l].astype(f32),
                     ml_n=state_mlstm_n[l].astype(f32),
                     ml_m=_pad_cols(state_mlstm_m[l].astype(f32), LANES)[:, None, :],
                     gla_s=state_gla_S[l].astype(f32)) for l in range(DEPTH)]
    y_sample, s_states = run_trunk(x_sample, s_states, sb=16, t_len=t_sample, tm=512, need_output=True)

    def stacked(states):
        st = lambda k: jnp.stack([s[k] for s in states])
        return (st("lru_h"), jnp.stack([_untail(s["lru_tail"]) for s in states]), st("ml_c"), st("ml_n"),
                st("ml_m")[:, :, 0, :ML_H], jnp.stack([_untail(s["ml_tail"]) for s in states]), st("gla_s"))

    return (y_prompt, y_sample) + stacked(p_states) + stacked(s_states)
```

```python
import functools

import jax
import jax.numpy as jnp
from jax import lax
from jax.experimental import pallas as pl
from jax.experimental.pallas import tpu as pltpu

f32 = jnp.float32
bf16 = jnp.bfloat16

D_MODEL = 1024
DEPTH = 2
N_META = 16
CONV_W = 4
CHUNK = 64
LRU_W = 256
LRU_BLOCKS = 4
LRU_C = 8.0
ML_H = 4
ML_D = 96
ML_W = ML_H * ML_D
GLA_H = 4
GLA_DK = 48
GLA_DV = 96
GLA_KW = GLA_H * GLA_DK
GLA_VW = GLA_H * GLA_DV
GLA_RANK = 16
GLA_TAU = 16.0
D_FF = 2816
EPS = 1e-6

SUBLANES = 8
LANES = 128
BF16_SUBLANES = 16
VMEM_LIMIT_BYTES = 60 * 1024 * 1024

FF_TILE = 256
N_FF_TILES = D_FF // FF_TILE
GLA_SUB = 16

HP = LANES
HW = ML_H * HP
ONE_LANE = ML_D
KP = 64
GLA_KP = GLA_H * KP

Z_UR = 0
Z_GR = Z_UR + LRU_W
Z_UM = Z_GR + LRU_W
Z_ZM = Z_UM + HW
Z_QG = Z_ZM + HW
Z_KG = Z_QG + GLA_KP
Z_VG = Z_KG + GLA_KP
Z_GG = Z_VG + HW
Z_AL = Z_GG + HW
Z_W = Z_AL + LANES


def _dot(a, b):
    return jnp.dot(a, b, preferred_element_type=f32)


def _dot_nt(a, b):
    return lax.dot_general(a, b, (((1,), (1,)), ((), ())), preferred_element_type=f32)


def _dot_tn(a, b):
    return lax.dot_general(a, b, (((0,), (0,)), ((), ())), preferred_element_type=f32)


def _rms(x, g):
    return x * lax.rsqrt(jnp.mean(x * x, axis=-1, keepdims=True) + EPS) * g


def _ffn_body(x_ref, g_ref, w1_ref, w3_ref, w2_ref, gf_ref, o_ref, gated_ref, *, post_norm):
    x = x_ref[...]
    h = _rms(x, g_ref[...]).astype(bf16)
    for f in range(N_FF_TILES):
        cols = slice(f * FF_TILE, (f + 1) * FF_TILE)
        a = _dot(h, w1_ref[:, cols])
        b = _dot(h, w3_ref[:, cols])
        gated_ref[:, cols] = (a * jax.nn.sigmoid(a) * b).astype(bf16)
    y = x + 0.5 * _dot(gated_ref[...], w2_ref[...])
    if post_norm:
        y = _rms(y, gf_ref[...])
    o_ref[...] = y


def _const_spec(shape):
    zeros = (0,) * len(shape)
    return pl.BlockSpec(shape, lambda *_: zeros, pipeline_mode=pl.Buffered(1))


def _ffn(x2d, g, w1, w3, w2, gf, *, tm, post_norm):
    rows = x2d.shape[0]
    return pl.pallas_call(
        functools.partial(_ffn_body, post_norm=post_norm),
        out_shape=jax.ShapeDtypeStruct(x2d.shape, f32),
        grid=(rows // tm,),
        in_specs=[
            pl.BlockSpec((tm, D_MODEL), lambda i: (i, 0)),
            _const_spec(g.shape),
            _const_spec(w1.shape),
            _const_spec(w3.shape),
            _const_spec(w2.shape),
            _const_spec(gf.shape),
        ],
        out_specs=pl.BlockSpec((tm, D_MODEL), lambda i: (i, 0)),
        scratch_shapes=[pltpu.VMEM((tm, D_FF), bf16)],
        compiler_params=pltpu.CompilerParams(
            dimension_semantics=("arbitrary",), vmem_limit_bytes=VMEM_LIMIT_BYTES),
        name="ffn",
    )(x2d, g, w1, w3, w2, gf)


def _row_time(shape, period):
    return lax.broadcasted_iota(jnp.int32, shape, 0) % period


def _seg_cumsum(x, period):
    t = _row_time(x.shape, period)
    s = 1
    while s < period:
        x = x + jnp.where(t >= s, pltpu.roll(x, s, 0), 0.0)
        s *= 2
    return x


def _rows_from_seq(v, t_len):
    sb, width = v.shape
    return jnp.broadcast_to(v[:, None, :], (sb, t_len, width)).reshape(sb * t_len, width)


def _last_rows(x, groups, period):
    return x.reshape(groups, period, x.shape[-1])[:, period - 1, :]


def _expand(x, e2):
    hi = x.astype(bf16)
    lo = (x - hi.astype(f32)).astype(bf16)
    return _dot(jnp.concatenate([hi, lo], axis=1), e2)


def _causal_conv(ext_ref, u, w_ref, b_ref, sb, t_len):
    width = u.shape[-1]
    ext_ref[:, SUBLANES:SUBLANES + t_len, :] = u.reshape(sb, t_len, width)
    w = w_ref[...]
    base = SUBLANES - (CONV_W - 1)
    out = b_ref[...] + ext_ref[:, base:base + t_len, :] * w[0:1, :]
    for j in range(1, CONV_W):
        out = out + ext_ref[:, base + j:base + j + t_len, :] * w[j:j + 1, :]
    ext_ref[:, 0:SUBLANES, :] = ext_ref[:, t_len:t_len + SUBLANES, :]
    return out.reshape(sb * t_len, width)


def _mix_body(x_ref, h0_ref, tr0_ref, tm0_ref, c0_ref, n0_ref, m0_ref, s0_ref,
              gmix_ref, win_ref, cwr_ref, cbr_ref, wa_ref, wx_ref, ba_ref, bx_ref, lam_ref,
              cwm_ref, cbm_ref, wq_ref, wk_ref, wv_ref, wif_ref, bif_ref,
              mlg_ref, mls_ref, wup_ref, bup_ref, glg_ref, wred_ref, wout_ref,
              e2_ref, ert2_ref, ones_ref, wms_ref,
              y_ref, h_ref, tro_ref, tmo_ref, c_ref, n_ref, m_ref, s_ref,
              extr, extm, z_ref, mq_ref, mk_ref, kw_ref, va_ref, rtx_ref, wix_ref, eix_ref, cc_ref,
              decx_ref, hm_ref, qt_ref, kh_ref, gdec_ref, og_ref, yr_ref, cm_ref, cst_ref,
              *, sb, t_len):
    chunk = pl.program_id(1)
    n_rows = sb * t_len
    eye_d = (lax.broadcasted_iota(jnp.int32, (ML_D, ML_D), 0)
             == lax.broadcasted_iota(jnp.int32, (ML_D, ML_D), 1))

    @pl.when(chunk == 0)
    def _load_state():
        h_ref[...] = h0_ref[...]
        extr[:, 0:SUBLANES, :] = tr0_ref[...]
        extm[:, 0:SUBLANES, :] = tm0_ref[...]
        m_ref[...] = m0_ref[...]
        s_ref[...] = s0_ref[...]

        def load(s, carry):
            cst_ref[s] = jnp.zeros((HP, HW), f32)
            for h in range(ML_H):
                cst_ref[s, 0:ML_D, h * HP:h * HP + ML_D] = c0_ref[s, h]
                n_col = jnp.sum(jnp.where(eye_d, n0_ref[s, h:h + 1, :], 0.0), axis=1, keepdims=True)
                cst_ref[s, 0:ML_D, h * HP + ONE_LANE:h * HP + ONE_LANE + 1] = n_col
            return carry

        lax.fori_loop(0, sb, load, 0)

    x = x_ref[...].reshape(n_rows, D_MODEL)
    hn = _rms(x, gmix_ref[...]).astype(bf16)
    z_ref[...] = _dot(hn, win_ref[...])

    xr = _causal_conv(extr, z_ref[:, Z_UR:Z_UR + LRU_W], cwr_ref, cbr_ref, sb, t_len)
    xr_b = xr.astype(bf16)
    r = jax.nn.sigmoid(_dot(xr_b, wa_ref[...]) + ba_ref[...])
    ig = jax.nn.sigmoid(_dot(xr_b, wx_ref[...]) + bx_ref[...])
    log_a = -LRU_C * r * jax.nn.softplus(-lam_ref[...])
    a = jnp.exp(log_a)
    bt = jnp.sqrt(1.0 - a * a) * (ig * xr)
    tl = _row_time((n_rows, LRU_W), t_len)
    s = 1
    while s < t_len:
        keep = tl >= s
        a_prev = jnp.where(keep, pltpu.roll(a, s, 0), 1.0)
        b_prev = jnp.where(keep, pltpu.roll(bt, s, 0), 0.0)
        bt = a * b_prev + bt
        a = a * a_prev
        s *= 2
    hs = bt + a * _rows_from_seq(h_ref[...], t_len)
    h_ref[...] = _last_rows(hs, sb, t_len)
    yr_ref[...] = jax.nn.gelu(z_ref[:, Z_GR:Z_GR + LRU_W]) * hs

    u_m = z_ref[:, Z_UM:Z_UM + HW]
    cm = jax.nn.silu(_causal_conv(extm, u_m, cwm_ref, cbm_ref, sb, t_len))
    cm_ref[...] = cm
    cm_b = cm.astype(bf16)
    mq = _dot(cm_b, wq_ref[...])
    mk = _dot(cm_b, wk_ref[...])
    va = _dot(u_m.astype(bf16), wv_ref[...]) + ones_ref[...]
    gates = _dot(jnp.concatenate([mq.astype(bf16), mk.astype(bf16), va.astype(bf16)], axis=1),
                 wif_ref[...]) + bif_ref[...]
    li = gates[:, 0:LANES]
    lf = jax.nn.log_sigmoid(gates[:, LANES:2 * LANES])
    tg = _row_time((n_rows, LANES), t_len)
    bcum, m_loc = lf, li
    s = 1
    while s < t_len:
        keep = tg >= s
        b_prev = jnp.where(keep, pltpu.roll(bcum, s, 0), 0.0)
        m_prev_seg = jnp.where(keep, pltpu.roll(m_loc, s, 0), -jnp.inf)
        m_loc = jnp.maximum(m_prev_seg + bcum, m_loc)
        bcum = bcum + b_prev
        s *= 2
    m_old = m_ref[:, 0, :]
    m_old_rows = _rows_from_seq(m_old, t_len)
    m_t = jnp.maximum(bcum + m_old_rows, m_loc)
    m_new = _last_rows(m_t, sb, t_len)
    b_last = _last_rows(bcum, sb, t_len)
    m_ref[...] = m_new[:, None, :]
    cc = li - bcum
    cc_ref[...] = cc
    e2 = e2_ref[...]
    rtx_ref[...] = _expand(bcum - m_t, ert2_ref[...])
    wix_ref[...] = jnp.exp(_expand(bcum + m_old_rows - m_t, e2))
    eix_ref[...] = jnp.exp(_expand(-m_t, e2))
    w_k = jnp.exp(_expand(_rows_from_seq(b_last, t_len) + cc - _rows_from_seq(m_new, t_len), e2))
    decx_ref[...] = jnp.exp(_expand(b_last + m_old - m_new, e2))[:, None, :]
    mq_ref[...] = (mq * (ML_D ** -0.5)).astype(mq_ref.dtype)
    mk_ref[...] = mk.astype(mk_ref.dtype)
    kw_ref[...] = (mk * w_k).astype(kw_ref.dtype)
    va_ref[...] = va.astype(va_ref.dtype)

    t4 = ML_H * t_len
    causal_cat = (lax.broadcasted_iota(jnp.int32, (t_len, t4), 0)
                  >= lax.broadcasted_iota(jnp.int32, (t_len, t4), 1) % t_len)
    stack_diag = (lax.broadcasted_iota(jnp.int32, (t4, HW), 0) // t_len
                  == lax.broadcasted_iota(jnp.int32, (t4, HW), 1) // HP)
    state_diag = (lax.broadcasted_iota(jnp.int32, (HW, HW), 0) // HP
                  == lax.broadcasted_iota(jnp.int32, (HW, HW), 1) // HP)
    lane_head = lax.broadcasted_iota(jnp.int32, (t_len, HW), 1) // HP

    def mlstm_seq(s, carry):
        rows = pl.ds(pl.multiple_of(s * t_len, t_len), t_len)
        q_b = mq_ref[rows, :].astype(bf16)
        k_bd = jnp.where(stack_diag, jnp.concatenate([mk_ref[rows, :]] * ML_H, axis=0), 0.0).astype(bf16)
        v_bd = jnp.where(stack_diag, jnp.concatenate([va_ref[rows, :]] * ML_H, axis=0), 0.0).astype(bf16)
        c_t = cc_ref[rows, :].T
        c_row = jnp.concatenate([c_t[h:h + 1, :] for h in range(ML_H)], axis=1)
        dmat = jnp.where(causal_cat, rtx_ref[rows, :] + c_row, -jnp.inf)
        sc = _dot_nt(q_b, k_bd) * jnp.exp(dmat)
        c_old = cst_ref[s]
        c_bd = jnp.where(state_diag, jnp.concatenate([c_old] * ML_H, axis=0), 0.0).astype(bf16)
        num = _dot(sc.astype(bf16), v_bd) + _dot(q_b, c_bd) * wix_ref[rows, :]
        den = jnp.zeros((t_len, HW), f32)
        for h in range(ML_H):
            den = jnp.where(lane_head == h, num[:, h * HP + ONE_LANE:h * HP + ONE_LANE + 1], den)
        hm_ref[rows, :] = num / jnp.maximum(jnp.abs(den), eix_ref[rows, :])
        kw_st = jnp.concatenate([kw_ref[rows, h * HP:(h + 1) * HP] for h in range(ML_H)],
                                axis=0).astype(bf16)
        cst_ref[s] = c_old * decx_ref[s] + _dot_tn(kw_st, v_bd)
        return carry

    lax.fori_loop(0, sb, mlstm_seq, 0, unroll=2)

    sub = min(GLA_SUB, t_len)
    n_sub = t_len // sub
    al_b = z_ref[:, Z_AL:Z_AL + LANES].astype(bf16)
    lg = jax.nn.log_sigmoid(_dot(al_b, wup_ref[...]) + bup_ref[...]) * (1.0 / GLA_TAU)
    bcl = _seg_cumsum(lg, sub)
    gq = z_ref[:, Z_QG:Z_QG + GLA_KP] * (GLA_DK ** -0.5)
    gk = z_ref[:, Z_KG:Z_KG + GLA_KP]
    gv = z_ref[:, Z_VG:Z_VG + HW]
    last = _last_rows(bcl, n_rows // sub, sub)
    gdec_ref[...] = jnp.exp(last)[:, None, :]
    qt_ref[...] = gq * jnp.exp(bcl)
    kh_ref[...] = gk * jnp.exp(_rows_from_seq(last, sub) - bcl)
    ts = _row_time((n_rows, GLA_KP), sub)
    o_intra = _dot((gq * gk).astype(bf16), wred_ref[...]) * gv
    for d in range(1, sub):
        e = jnp.where(ts >= d, gq * pltpu.roll(gk, d, 0) * jnp.exp(bcl - pltpu.roll(bcl, d, 0)), 0.0)
        o_intra = o_intra + _dot(e.astype(bf16), wred_ref[...]) * pltpu.roll(gv, d, 0)
    og_ref[...] = o_intra

    stack_head = lax.broadcasted_iota(jnp.int32, (GLA_H * sub, GLA_KP), 0) // sub
    lane_head = lax.broadcasted_iota(jnp.int32, (GLA_H * sub, GLA_KP), 1) // KP
    head_diag = stack_head == lane_head

    def gla_seq(s, carry):
        st = s_ref[s]
        for j in range(n_sub):
            rows = pl.ds(pl.multiple_of(s * t_len + j * sub, sub), sub)
            q_bd = jnp.where(head_diag, jnp.concatenate([qt_ref[rows, :]] * GLA_H, axis=0), 0.0).astype(bf16)
            k_bd = jnp.where(head_diag, jnp.concatenate([kh_ref[rows, :]] * GLA_H, axis=0), 0.0).astype(bf16)
            v_st = jnp.concatenate([z_ref[rows, Z_VG + h * HP:Z_VG + (h + 1) * HP] for h in range(GLA_H)],
                                   axis=0).astype(bf16)
            o4 = _dot_nt(q_bd, st.astype(bf16))
            for h in range(GLA_H):
                lanes = slice(h * HP, (h + 1) * HP)
                og_ref[rows, lanes] = og_ref[rows, lanes] + o4[h * sub:(h + 1) * sub, :]
            st = st * gdec_ref[s * n_sub + j] + _dot_tn(v_st, k_bd)
        s_ref[s] = st
        return carry

    lax.fori_loop(0, sb, gla_seq, 0, unroll=2)

    def head_norm(o):
        ms = _dot((o * o).astype(bf16), wms_ref[...]) * (1.0 / ML_D)
        return o * lax.rsqrt(ms + EPS)

    y_m = (jax.nn.sigmoid(z_ref[:, Z_ZM:Z_ZM + HW])
           * (head_norm(hm_ref[...]) * mlg_ref[...] + mls_ref[...] * cm_ref[...]))
    y_g = head_norm(og_ref[...]) * glg_ref[...] * jax.nn.silu(z_ref[:, Z_GG:Z_GG + HW])
    y = (_dot(yr_ref[...].astype(bf16), wout_ref[0:LRU_W, :])
         + _dot(y_m.astype(bf16), wout_ref[LRU_W:LRU_W + HW, :])
         + _dot(y_g.astype(bf16), wout_ref[LRU_W + HW:, :]))
    y_ref[...] = x_ref[...] + y.reshape(sb, t_len, D_MODEL)

    @pl.when(chunk == pl.num_programs(1) - 1)
    def _store_state():
        tro_ref[...] = extr[:, 0:SUBLANES, :]
        tmo_ref[...] = extm[:, 0:SUBLANES, :]

        def store(s, carry):
            for h in range(ML_H):
                c_ref[s, h] = cst_ref[s, 0:ML_D, h * HP:h * HP + ML_D]
                n_col = cst_ref[s, 0:ML_D, h * HP + ONE_LANE:h * HP + ONE_LANE + 1]
                n_ref[s, h:h + 1, :] = jnp.sum(jnp.where(eye_d, n_col, 0.0), axis=0, keepdims=True)
            return carry

        lax.fori_loop(0, sb, store, 0)


_STATE_KEYS = ("lru_h", "lru_tail", "ml_tail", "ml_c", "ml_n", "ml_m", "gla_s")
_MIX_WEIGHT_KEYS = ("mix_g", "w_in", "lru_cw", "lru_cb", "lru_wa", "lru_wx", "lru_ba", "lru_bx", "lru_lam",
                    "ml_cw", "ml_cb", "ml_wq", "ml_wk", "ml_wv", "ml_wif", "ml_bif",
                    "ml_norm_g", "ml_skip", "gla_wup", "gla_bup", "gla_norm_g", "gla_wred", "w_out")


def _mix(x3d, state, lw, consts, *, sb, t_len):
    n_seq, length, _ = x3d.shape
    n_rows = sb * t_len
    n_sub = t_len // min(GLA_SUB, t_len)
    grid = (n_seq // sb, length // t_len)

    def seq_spec(arr):
        blk = (sb,) + arr.shape[1:]
        zeros = (0,) * (arr.ndim - 1)
        return pl.BlockSpec(blk, lambda i, c: (i,) + zeros, pipeline_mode=pl.Buffered(1))

    states = [state[k] for k in _STATE_KEYS]
    weights = [lw[k] for k in _MIX_WEIGHT_KEYS] + list(consts)
    x_spec = pl.BlockSpec((sb, t_len, D_MODEL), lambda i, c: (i, c, 0))
    row_scratch = lambda width, dtype=f32: pltpu.VMEM((n_rows, width), dtype)
    opnd = bf16 if t_len % BF16_SUBLANES == 0 else f32
    outs = pl.pallas_call(
        functools.partial(_mix_body, sb=sb, t_len=t_len),
        out_shape=[jax.ShapeDtypeStruct(x3d.shape, f32)] + [jax.ShapeDtypeStruct(a.shape, f32) for a in states],
        grid=grid,
        in_specs=[x_spec] + [seq_spec(a) for a in states] + [_const_spec(w.shape) for w in weights],
        out_specs=[x_spec] + [seq_spec(a) for a in states],
        scratch_shapes=[
            pltpu.VMEM((sb, t_len + SUBLANES, LRU_W), f32),
            pltpu.VMEM((sb, t_len + SUBLANES, HW), f32),
            row_scratch(Z_W),
            row_scratch(HW, opnd),
            row_scratch(HW, opnd),
            row_scratch(HW, opnd),
            row_scratch(HW, opnd),
            row_scratch(ML_H * t_len),
            row_scratch(HW),
            row_scratch(HW),
            row_scratch(LANES),
            pltpu.VMEM((sb, 1, HW), f32),
            row_scratch(HW),
            row_scratch(GLA_KP),
            row_scratch(GLA_KP),
            pltpu.VMEM((sb * n_sub, 1, GLA_KP), f32),
            row_scratch(HW),
            row_scratch(LRU_W),
            row_scratch(HW),
            pltpu.VMEM((sb, HP, HW), f32),
        ],
        compiler_params=pltpu.CompilerParams(
            dimension_semantics=("arbitrary", "arbitrary"), vmem_limit_bytes=VMEM_LIMIT_BYTES),
        name="mixer",
    )(x3d, *states, *weights)
    return outs[0], dict(zip(_STATE_KEYS, outs[1:]))


def _pad_heads(w, d, dp):
    lead = w.shape[:-1]
    heads = w.shape[-1] // d
    w = w.reshape(lead + (heads, d))
    w = jnp.pad(w, [(0, 0)] * (len(lead) + 1) + [(0, dp - d)])
    return w.reshape(lead + (heads * dp,))


def _pad_head_rows(w, d, dp):
    return _pad_heads(w.T, d, dp).T


def _block_diag(blocks, dp=None):
    n, d_in, d_out = blocks.shape
    dp = d_in if dp is None else dp
    out = jnp.zeros((n * dp, n * dp), blocks.dtype)
    for i in range(n):
        out = out.at[i * dp:i * dp + d_in, i * dp:i * dp + d_out].set(blocks[i])
    return out


def _pad_cols(w, width):
    return jnp.pad(w, ((0, 0), (0, width - w.shape[1])))


def _row(v, width=None):
    v = v.reshape(1, -1).astype(f32)
    return v if width is None else _pad_cols(v, width)


def _tail(conv_state, d=None, dp=None):
    conv_state = conv_state.astype(f32)
    if d is not None:
        conv_state = _pad_heads(conv_state, d, dp)
    return jnp.pad(conv_state, ((0, 0), (SUBLANES - (CONV_W - 1), 0), (0, 0)))


def _untail(tail, d=None, dp=None):
    tail = tail[:, SUBLANES - (CONV_W - 1):, :]
    if d is not None:
        s, r, _ = tail.shape
        tail = tail.reshape(s, r, -1, dp)[..., :d].reshape(s, r, -1)
    return tail


def _gla_state_in(s):
    s = jnp.transpose(s.astype(f32), (0, 3, 1, 2))
    s = jnp.pad(s, ((0, 0), (0, HP - GLA_DV), (0, 0), (0, KP - GLA_DK)))
    return s.reshape(s.shape[0], HP, GLA_KP)


def _gla_state_out(st):
    st = st.reshape(st.shape[0], HP, GLA_H, KP)[:, :GLA_DV, :, :GLA_DK]
    return jnp.transpose(st, (0, 2, 3, 1))


def _mix_consts(t_len):
    head = jnp.arange(LANES)[:, None]
    lane_head = (jnp.arange(HW) // HP)[None, :]
    e = (head == lane_head).astype(bf16)
    e2 = jnp.concatenate([e, e], axis=0)
    e_rt = (head == (jnp.arange(ML_H * t_len) // t_len)[None, :]).astype(bf16)
    ert2 = jnp.concatenate([e_rt, e_rt], axis=0)
    lane = jnp.arange(HW)
    ones_row = ((lane % HP) == ONE_LANE).astype(f32)[None, :]
    real = (lane % HP) < ML_D
    w_ms = (((lane // HP)[:, None] == (lane // HP)[None, :]) & real[:, None]).astype(bf16)
    return e2, ert2, ones_row, w_ms


def kernel(x_prompt, x_sample, state_lru_h, state_lru_conv, state_mlstm_C, state_mlstm_n, state_mlstm_m, state_mlstm_conv, state_gla_S, meta_tokens, ffn1_norm_g, ffn1_w1, ffn1_w3, ffn1_w2, mix_norm_g, w_in, lru_conv_w, lru_conv_b, lru_wa, lru_ba, lru_wx, lru_bx, lru_lambda, ml_conv_w, ml_conv_b, ml_wq, ml_wk, ml_wv, ml_w_if, ml_b_if, ml_norm_g, ml_skip, gla_w_up, gla_b_up, gla_norm_g, w_out, ffn2_norm_g, ffn2_w1, ffn2_w3, ffn2_w2, final_norm_g):
    n_prompt = x_prompt.shape[0]
    n_sample, t_sample, _ = x_sample.shape

    k_lane = jnp.arange(GLA_KP)
    v_lane = jnp.arange(HW)
    wred = (((k_lane // KP)[:, None] == (v_lane // HP)[None, :])
            & ((k_lane % KP) < GLA_DK)[:, None]).astype(bf16)

    layers = []
    for l in range(DEPTH):
        w = w_in[l]
        sizes = (LRU_W, LRU_W, ML_W, ML_W, GLA_KW, GLA_KW, GLA_VW, GLA_VW, GLA_RANK)
        pads = (None, None, (ML_D, HP), (ML_D, HP), (GLA_DK, KP), (GLA_DK, KP), (GLA_DV, HP), (GLA_DV, HP), None)
        parts, off = [], 0
        for size, pad in zip(sizes, pads):
            part = w[:, off:off + size]
            parts.append(part if pad is None else _pad_heads(part, *pad))
            off += size
        parts[-1] = _pad_cols(parts[-1], LANES)
        wif = ml_w_if[l]
        wif = jnp.concatenate([_pad_head_rows(wif[i * ML_W:(i + 1) * ML_W], ML_D, HP) for i in range(3)], axis=0)
        wif = jnp.concatenate([_pad_cols(wif[:, :ML_H], LANES), _pad_cols(wif[:, ML_H:], LANES)], axis=1)
        bif = jnp.concatenate([_row(ml_b_if[l][:ML_H], LANES), _row(ml_b_if[l][ML_H:], LANES)], axis=1)
        wo = w_out[l]
        wo = jnp.concatenate([wo[:LRU_W], _pad_head_rows(wo[LRU_W:LRU_W + ML_W], ML_D, HP),
                              _pad_head_rows(wo[LRU_W + ML_W:], GLA_DV, HP)], axis=0)
        wup = _pad_heads(gla_w_up[l], GLA_DK, KP)
        layers.append(dict(
            f1_g=_row(ffn1_norm_g[l]), f1_w1=ffn1_w1[l].astype(bf16), f1_w3=ffn1_w3[l].astype(bf16),
            f1_w2=ffn1_w2[l].astype(bf16),
            f2_g=_row(ffn2_norm_g[l]), f2_w1=ffn2_w1[l].astype(bf16), f2_w3=ffn2_w3[l].astype(bf16),
            f2_w2=ffn2_w2[l].astype(bf16),
            mix_g=_row(mix_norm_g[l]),
            w_in=jnp.concatenate(parts, axis=1).astype(bf16),
            lru_cw=lru_conv_w[l].astype(f32), lru_cb=_row(lru_conv_b[l]),
            lru_wa=_block_diag(lru_wa[l]).astype(bf16), lru_wx=_block_diag(lru_wx[l]).astype(bf16),
            lru_ba=_row(lru_ba[l]), lru_bx=_row(lru_bx[l]), lru_lam=_row(lru_lambda[l]),
            ml_cw=_pad_heads(ml_conv_w[l].astype(f32), ML_D, HP), ml_cb=_pad_heads(_row(ml_conv_b[l]), ML_D, HP),
            ml_wq=_block_diag(ml_wq[l], HP).astype(bf16), ml_wk=_block_diag(ml_wk[l], HP).astype(bf16),
            ml_wv=_block_diag(ml_wv[l], HP).astype(bf16),
            ml_wif=wif.astype(bf16), ml_bif=bif,
            ml_norm_g=_pad_heads(_row(ml_norm_g[l]), ML_D, HP), ml_skip=_pad_heads(_row(ml_skip[l]), ML_D, HP),
            gla_wup=jnp.pad(wup, ((0, LANES - GLA_RANK), (0, 0))).astype(bf16),
            gla_bup=_pad_heads(_row(gla_b_up[l]), GLA_DK, KP),
            gla_norm_g=_pad_heads(_row(gla_norm_g[l]), GLA_DV, HP),
            gla_wred=wred, w_out=wo.astype(bf16),
        ))
    final_g = _row(final_norm_g)

    def run_trunk(x3d, states, *, sb, t_len, tm, need_output):
        n_seq, length, _ = x3d.shape
        consts = _mix_consts(t_len)
        new_states = []
        for l in range(DEPTH):
            lw = layers[l]
            x2d = _ffn(x3d.reshape(n_seq * length, D_MODEL), lw["f1_g"], lw["f1_w1"], lw["f1_w3"], lw["f1_w2"], final_g,
                       tm=tm, post_norm=False)
            x3d, st = _mix(x2d.reshape(n_seq, length, D_MODEL), states[l], lw, consts, sb=sb, t_len=t_len)
            new_states.append(st)
            last = l == DEPTH - 1
            if need_output or not last:
                x2d = _ffn(x3d.reshape(n_seq * length, D_MODEL), lw["f2_g"], lw["f2_w1"], lw["f2_w3"], lw["f2_w2"], final_g,
                           tm=tm, post_norm=last)
                x3d = x2d.reshape(n_seq, length, D_MODEL)
        return x3d, new_states

    zero_state = dict(
        lru_h=jnp.zeros((n_prompt, LRU_W), f32), lru_tail=jnp.zeros((n_prompt, SUBLANES, LRU_W), f32),
        ml_tail=jnp.zeros((n_prompt, SUBLANES, HW), f32), ml_c=jnp.zeros((n_prompt, ML_H, ML_D, ML_D), f32),
        ml_n=jnp.zeros((n_prompt, ML_H, ML_D), f32), ml_m=jnp.zeros((n_prompt, 1, LANES), f32),
        gla_s=jnp.zeros((n_prompt, HP, GLA_KP), f32))
    x_meta = jnp.broadcast_to(meta_tokens.astype(f32)[None], (n_prompt, N_META, D_MODEL))
    _, meta_states = run_trunk(x_meta, [zero_state] * DEPTH, sb=n_prompt, t_len=N_META,
                               tm=n_prompt * N_META, need_output=False)

    y_prompt, p_states = run_trunk(x_prompt, meta_states, sb=n_prompt, t_len=CHUNK, tm=512, need_output=True)

    s_states = [dict(lru_h=state_lru_h[l].astype(f32), lru_tail=_tail(state_lru_conv[l]),
                     ml_tail=_tail(state_mlstm_conv[l], ML_D, HP), ml_c=state_mlstm_C[l].astype(f32),
                     ml_n=state_mlstm_n[l].astype(f32),
                     ml_m=_pad_cols(state_mlstm_m[l].astype(f32), LANES)[:, None, :],
                     gla_s=_gla_state_in(state_gla_S[l])) for l in range(DEPTH)]
    y_sample, s_states = run_trunk(x_sample, s_states, sb=16, t_len=t_sample, tm=512, need_output=True)

    def stacked(states):
        st = lambda k: jnp.stack([s[k] for s in states])
        return (st("lru_h"), jnp.stack([_untail(s["lru_tail"]) for s in states]), st("ml_c"), st("ml_n"),
                st("ml_m")[:, :, 0, :ML_H], jnp.stack([_untail(s["ml_tail"], ML_D, HP) for s in states]),
                jnp.stack([_gla_state_out(s["gla_s"]) for s in states]))

    return (y_prompt, y_sample) + stacked(p_states) + stacked(s_states)
```

```python
import functools

import jax
import jax.numpy as jnp
from jax import lax
from jax.experimental import pallas as pl
from jax.experimental.pallas import tpu as pltpu

f32 = jnp.float32
bf16 = jnp.bfloat16

D_MODEL = 1024
DEPTH = 2
N_META = 16
CONV_W = 4
CHUNK = 64
LRU_W = 256
LRU_BLOCKS = 4
LRU_C = 8.0
ML_H = 4
ML_D = 96
ML_W = ML_H * ML_D
GLA_H = 4
GLA_DK = 48
GLA_DV = 96
GLA_KW = GLA_H * GLA_DK
GLA_VW = GLA_H * GLA_DV
GLA_RANK = 16
GLA_TAU = 16.0
D_FF = 2816
EPS = 1e-6

SUBLANES = 8
LANES = 128
BF16_SUBLANES = 16
VMEM_LIMIT_BYTES = 60 * 1024 * 1024

FF_TILE = 256
N_FF_TILES = D_FF // FF_TILE
GLA_SUB = 16

HP = LANES
HW = ML_H * HP
ONE_LANE = ML_D
KP = 64
GLA_KP = GLA_H * KP

Z_UR = 0
Z_GR = Z_UR + LRU_W
Z_UM = Z_GR + LRU_W
Z_ZM = Z_UM + HW
Z_QG = Z_ZM + HW
Z_KG = Z_QG + GLA_KP
Z_VG = Z_KG + GLA_KP
Z_GG = Z_VG + HW
Z_AL = Z_GG + HW
Z_W = Z_AL + LANES


def _dot(a, b):
    return jnp.dot(a, b, preferred_element_type=f32)


def _dot_nt(a, b):
    return lax.dot_general(a, b, (((1,), (1,)), ((), ())), preferred_element_type=f32)


def _dot_tn(a, b):
    return lax.dot_general(a, b, (((0,), (0,)), ((), ())), preferred_element_type=f32)


def _rms(x, g):
    return x * lax.rsqrt(jnp.mean(x * x, axis=-1, keepdims=True) + EPS) * g


def _ffn_body(x_ref, g_ref, w1_ref, w3_ref, w2_ref, gf_ref, o_ref, gated_ref, *, post_norm):
    x = x_ref[...]
    h = _rms(x, g_ref[...]).astype(bf16)
    for f in range(N_FF_TILES):
        cols = slice(f * FF_TILE, (f + 1) * FF_TILE)
        a = _dot(h, w1_ref[:, cols])
        b = _dot(h, w3_ref[:, cols])
        gated_ref[:, cols] = (a * jax.nn.sigmoid(a) * b).astype(bf16)
    y = x + 0.5 * _dot(gated_ref[...], w2_ref[...])
    if post_norm:
        y = _rms(y, gf_ref[...])
    o_ref[...] = y


def _const_spec(shape):
    zeros = (0,) * len(shape)
    return pl.BlockSpec(shape, lambda *_: zeros, pipeline_mode=pl.Buffered(1))


def _layer_spec(shape, layer):
    zeros = (0,) * (len(shape) - 1)
    return pl.BlockSpec((None,) + tuple(shape[1:]), lambda *_: (layer,) + zeros, pipeline_mode=pl.Buffered(1))


def _ffn(x2d, g, w1, w3, w2, gf, *, layer, tm, post_norm):
    rows = x2d.shape[0]
    return pl.pallas_call(
        functools.partial(_ffn_body, post_norm=post_norm),
        out_shape=jax.ShapeDtypeStruct(x2d.shape, f32),
        grid=(rows // tm,),
        in_specs=[
            pl.BlockSpec((tm, D_MODEL), lambda i: (i, 0)),
            _layer_spec(g.shape, layer),
            _layer_spec(w1.shape, layer),
            _layer_spec(w3.shape, layer),
            _layer_spec(w2.shape, layer),
            _const_spec(gf.shape),
        ],
        out_specs=pl.BlockSpec((tm, D_MODEL), lambda i: (i, 0)),
        scratch_shapes=[pltpu.VMEM((tm, D_FF), bf16)],
        compiler_params=pltpu.CompilerParams(
            dimension_semantics=("arbitrary",), vmem_limit_bytes=VMEM_LIMIT_BYTES),
        name="ffn",
    )(x2d, g, w1, w3, w2, gf)


def _row_time(shape, period):
    return lax.broadcasted_iota(jnp.int32, shape, 0) % period


def _seg_cumsum(x, period):
    t = _row_time(x.shape, period)
    s = 1
    while s < period:
        x = x + jnp.where(t >= s, pltpu.roll(x, s, 0), 0.0)
        s *= 2
    return x


def _rows_from_seq(v, t_len):
    sb, width = v.shape
    return jnp.broadcast_to(v[:, None, :], (sb, t_len, width)).reshape(sb * t_len, width)


def _last_rows(x, groups, period):
    return x.reshape(groups, period, x.shape[-1])[:, period - 1, :]


def _expand(x, e2):
    hi = x.astype(bf16)
    lo = (x - hi.astype(f32)).astype(bf16)
    return _dot(jnp.concatenate([hi, lo], axis=1), e2)


def _causal_conv(ext_ref, u, w, b, sb, t_len):
    width = u.shape[-1]
    ext_ref[:, SUBLANES:SUBLANES + t_len, :] = u.reshape(sb, t_len, width)
    base = SUBLANES - (CONV_W - 1)
    out = b + ext_ref[:, base:base + t_len, :] * w[0:1, :]
    for j in range(1, CONV_W):
        out = out + ext_ref[:, base + j:base + j + t_len, :] * w[j:j + 1, :]
    ext_ref[:, 0:SUBLANES, :] = ext_ref[:, t_len:t_len + SUBLANES, :]
    return out.reshape(sb * t_len, width)


V_MIX_G = 0
V_LRU_CW = 1
V_LRU_CB = 5
V_LRU_BA = 6
V_LRU_BX = 7
V_LRU_LAM = 8
V_ML_CW = 9
V_ML_CB = 13
V_ML_BIF = 14
V_ML_G = 15
V_ML_SKIP = 16
V_GLA_BUP = 17
V_GLA_G = 18
V_ONES = 19
VEC_ROWS = 24
N_STATES = 7
_MIX_WEIGHT_KEYS = ("vecs", "w_in", "lru_wa", "lru_wx", "ml_wq", "ml_wk", "ml_wv", "ml_wif", "gla_wup", "w_out")
N_MIX_CONSTS = 4
N_MIX_INPUTS = 1 + N_STATES + len(_MIX_WEIGHT_KEYS) + N_MIX_CONSTS


def _mix_body(*refs, sb, t_len, n_alias):
    (x_ref, h0_ref, tr0_ref, tm0_ref, c0_ref, n0_ref, m0_ref, s0_ref,
     vec_ref, win_ref, wa_ref, wx_ref, wq_ref, wk_ref, wv_ref, wif_ref, wup_ref, wout_ref,
     wred_ref, e2_ref, ert2_ref, wms_ref) = refs[:N_MIX_INPUTS]
    (y_ref, h_ref, tro_ref, tmo_ref, c_ref, n_ref, m_ref, s_ref,
     extr, extm, z_ref, mq_ref, mk_ref, kw_ref, va_ref, rtx_ref, wix_ref, eix_ref, cc_ref,
     decx_ref, hm_ref, qt_ref, kh_ref, gdec_ref, og_ref, yr_ref, cm_ref, cst_ref) = refs[N_MIX_INPUTS + n_alias:]

    def vec(row, width, n=1):
        return vec_ref[row:row + n, 0:width]

    chunk = pl.program_id(1)
    n_rows = sb * t_len
    eye_d = (lax.broadcasted_iota(jnp.int32, (ML_D, ML_D), 0)
             == lax.broadcasted_iota(jnp.int32, (ML_D, ML_D), 1))

    @pl.when(chunk == 0)
    def _load_state():
        h_ref[...] = h0_ref[...]
        extr[:, 0:SUBLANES, :] = tr0_ref[...]
        extm[:, 0:SUBLANES, :] = tm0_ref[...]
        m_ref[...] = m0_ref[...]
        s_ref[...] = s0_ref[...]

        def load(s, carry):
            cst_ref[s] = jnp.zeros((HP, HW), f32)
            for h in range(ML_H):
                cst_ref[s, 0:ML_D, h * HP:h * HP + ML_D] = c0_ref[s, h]
                n_col = jnp.sum(jnp.where(eye_d, n0_ref[s, h:h + 1, :], 0.0), axis=1, keepdims=True)
                cst_ref[s, 0:ML_D, h * HP + ONE_LANE:h * HP + ONE_LANE + 1] = n_col
            return carry

        lax.fori_loop(0, sb, load, 0)

    x = x_ref[...].reshape(n_rows, D_MODEL)
    hn = _rms(x, vec(V_MIX_G, D_MODEL)).astype(bf16)
    z_ref[...] = _dot(hn, win_ref[...])

    xr = _causal_conv(extr, z_ref[:, Z_UR:Z_UR + LRU_W], vec(V_LRU_CW, LRU_W, CONV_W), vec(V_LRU_CB, LRU_W),
                      sb, t_len)
    xr_b = xr.astype(bf16)
    r = jax.nn.sigmoid(_dot(xr_b, wa_ref[...]) + vec(V_LRU_BA, LRU_W))
    ig = jax.nn.sigmoid(_dot(xr_b, wx_ref[...]) + vec(V_LRU_BX, LRU_W))
    log_a = -LRU_C * r * jax.nn.softplus(-vec(V_LRU_LAM, LRU_W))
    a = jnp.exp(log_a)
    bt = jnp.sqrt(1.0 - a * a) * (ig * xr)
    tl = _row_time((n_rows, LRU_W), t_len)
    s = 1
    while s < t_len:
        keep = tl >= s
        a_prev = jnp.where(keep, pltpu.roll(a, s, 0), 1.0)
        b_prev = jnp.where(keep, pltpu.roll(bt, s, 0), 0.0)
        bt = a * b_prev + bt
        a = a * a_prev
        s *= 2
    hs = bt + a * _rows_from_seq(h_ref[...], t_len)
    h_ref[...] = _last_rows(hs, sb, t_len)
    yr_ref[...] = jax.nn.gelu(z_ref[:, Z_GR:Z_GR + LRU_W]) * hs

    u_m = z_ref[:, Z_UM:Z_UM + HW]
    cm = jax.nn.silu(_causal_conv(extm, u_m, vec(V_ML_CW, HW, CONV_W), vec(V_ML_CB, HW), sb, t_len))
    cm_ref[...] = cm
    cm_b = cm.astype(bf16)
    mq = _dot(cm_b, wq_ref[...])
    mk = _dot(cm_b, wk_ref[...])
    va = _dot(u_m.astype(bf16), wv_ref[...]) + vec(V_ONES, HW)
    gates = _dot(jnp.concatenate([mq.astype(bf16), mk.astype(bf16), va.astype(bf16)], axis=1),
                 wif_ref[...]) + vec(V_ML_BIF, 2 * LANES)
    li = gates[:, 0:LANES]
    lf = jax.nn.log_sigmoid(gates[:, LANES:2 * LANES])
    tg = _row_time((n_rows, LANES), t_len)
    bcum, m_loc = lf, li
    s = 1
    while s < t_len:
        keep = tg >= s
        b_prev = jnp.where(keep, pltpu.roll(bcum, s, 0), 0.0)
        m_prev_seg = jnp.where(keep, pltpu.roll(m_loc, s, 0), -jnp.inf)
        m_loc = jnp.maximum(m_prev_seg + bcum, m_loc)
        bcum = bcum + b_prev
        s *= 2
    m_old = m_ref[:, 0, :]
    m_old_rows = _rows_from_seq(m_old, t_len)
    m_t = jnp.maximum(bcum + m_old_rows, m_loc)
    m_new = _last_rows(m_t, sb, t_len)
    b_last = _last_rows(bcum, sb, t_len)
    m_ref[...] = m_new[:, None, :]
    cc = li - bcum
    cc_ref[...] = cc
    e2 = e2_ref[...]
    rtx_ref[...] = _expand(bcum - m_t, ert2_ref[...])
    wix_ref[...] = jnp.exp(_expand(bcum + m_old_rows - m_t, e2))
    eix_ref[...] = jnp.exp(_expand(-m_t, e2))
    w_k = jnp.exp(_expand(_rows_from_seq(b_last, t_len) + cc - _rows_from_seq(m_new, t_len), e2))
    decx_ref[...] = jnp.exp(_expand(b_last + m_old - m_new, e2))[:, None, :]
    mq_ref[...] = (mq * (ML_D ** -0.5)).astype(mq_ref.dtype)
    mk_ref[...] = mk.astype(mk_ref.dtype)
    kw_ref[...] = (mk * w_k).astype(kw_ref.dtype)
    va_ref[...] = va.astype(va_ref.dtype)

    t4 = ML_H * t_len
    causal_cat = (lax.broadcasted_iota(jnp.int32, (t_len, t4), 0)
                  >= lax.broadcasted_iota(jnp.int32, (t_len, t4), 1) % t_len)
    stack_diag = (lax.broadcasted_iota(jnp.int32, (t4, HW), 0) // t_len
                  == lax.broadcasted_iota(jnp.int32, (t4, HW), 1) // HP)
    state_diag = (lax.broadcasted_iota(jnp.int32, (HW, HW), 0) // HP
                  == lax.broadcasted_iota(jnp.int32, (HW, HW), 1) // HP)
    lane_head = lax.broadcasted_iota(jnp.int32, (t_len, HW), 1) // HP

    def mlstm_seq(s, carry):
        rows = pl.ds(pl.multiple_of(s * t_len, t_len), t_len)
        q_b = mq_ref[rows, :].astype(bf16)
        k_bd = jnp.where(stack_diag, jnp.concatenate([mk_ref[rows, :]] * ML_H, axis=0), 0.0).astype(bf16)
        v_bd = jnp.where(stack_diag, jnp.concatenate([va_ref[rows, :]] * ML_H, axis=0), 0.0).astype(bf16)
        c_t = cc_ref[rows, :].T
        c_row = jnp.concatenate([c_t[h:h + 1, :] for h in range(ML_H)], axis=1)
        dmat = jnp.where(causal_cat, rtx_ref[rows, :] + c_row, -jnp.inf)
        sc = _dot_nt(q_b, k_bd) * jnp.exp(dmat)
        c_old = cst_ref[s]
        c_bd = jnp.where(state_diag, jnp.concatenate([c_old] * ML_H, axis=0), 0.0).astype(bf16)
        num = _dot(sc.astype(bf16), v_bd) + _dot(q_b, c_bd) * wix_ref[rows, :]
        den = jnp.zeros((t_len, HW), f32)
        for h in range(ML_H):
            den = jnp.where(lane_head == h, num[:, h * HP + ONE_LANE:h * HP + ONE_LANE + 1], den)
        hm_ref[rows, :] = num / jnp.maximum(jnp.abs(den), eix_ref[rows, :])
        kw_st = jnp.concatenate([kw_ref[rows, h * HP:(h + 1) * HP] for h in range(ML_H)],
                                axis=0).astype(bf16)
        cst_ref[s] = c_old * decx_ref[s] + _dot_tn(kw_st, v_bd)
        return carry

    lax.fori_loop(0, sb, mlstm_seq, 0, unroll=2)

    sub = min(GLA_SUB, t_len)
    n_sub = t_len // sub
    al_b = z_ref[:, Z_AL:Z_AL + LANES].astype(bf16)
    lg = jax.nn.log_sigmoid(_dot(al_b, wup_ref[...]) + vec(V_GLA_BUP, GLA_KP)) * (1.0 / GLA_TAU)
    bcl = _seg_cumsum(lg, sub)
    gq = z_ref[:, Z_QG:Z_QG + GLA_KP] * (GLA_DK ** -0.5)
    gk = z_ref[:, Z_KG:Z_KG + GLA_KP]
    gv = z_ref[:, Z_VG:Z_VG + HW]
    last = _last_rows(bcl, n_rows // sub, sub)
    gdec_ref[...] = jnp.exp(last)[:, None, :]
    qt_ref[...] = gq * jnp.exp(bcl)
    kh_ref[...] = gk * jnp.exp(_rows_from_seq(last, sub) - bcl)
    ts = _row_time((n_rows, GLA_KP), sub)
    o_intra = _dot((gq * gk).astype(bf16), wred_ref[...]) * gv
    for d in range(1, sub):
        e = jnp.where(ts >= d, gq * pltpu.roll(gk, d, 0) * jnp.exp(bcl - pltpu.roll(bcl, d, 0)), 0.0)
        o_intra = o_intra + _dot(e.astype(bf16), wred_ref[...]) * pltpu.roll(gv, d, 0)
    og_ref[...] = o_intra

    stack_head = lax.broadcasted_iota(jnp.int32, (GLA_H * sub, GLA_KP), 0) // sub
    key_head = lax.broadcasted_iota(jnp.int32, (GLA_H * sub, GLA_KP), 1) // KP
    head_diag = stack_head == key_head

    def gla_seq(s, carry):
        st = s_ref[s]
        for j in range(n_sub):
            rows = pl.ds(pl.multiple_of(s * t_len + j * sub, sub), sub)
            q_bd = jnp.where(head_diag, jnp.concatenate([qt_ref[rows, :]] * GLA_H, axis=0), 0.0).astype(bf16)
            k_bd = jnp.where(head_diag, jnp.concatenate([kh_ref[rows, :]] * GLA_H, axis=0), 0.0).astype(bf16)
            v_st = jnp.concatenate([z_ref[rows, Z_VG + h * HP:Z_VG + (h + 1) * HP] for h in range(GLA_H)],
                                   axis=0).astype(bf16)
            o4 = _dot_nt(q_bd, st.astype(bf16))
            for h in range(GLA_H):
                lanes = slice(h * HP, (h + 1) * HP)
                og_ref[rows, lanes] = og_ref[rows, lanes] + o4[h * sub:(h + 1) * sub, :]
            st = st * gdec_ref[s * n_sub + j] + _dot_tn(v_st, k_bd)
        s_ref[s] = st
        return carry

    lax.fori_loop(0, sb, gla_seq, 0, unroll=2)

    def head_norm(o):
        ms = _dot((o * o).astype(bf16), wms_ref[...]) * (1.0 / ML_D)
        return o * lax.rsqrt(ms + EPS)

    y_m = (jax.nn.sigmoid(z_ref[:, Z_ZM:Z_ZM + HW])
           * (head_norm(hm_ref[...]) * vec(V_ML_G, HW) + vec(V_ML_SKIP, HW) * cm_ref[...]))
    y_g = head_norm(og_ref[...]) * vec(V_GLA_G, HW) * jax.nn.silu(z_ref[:, Z_GG:Z_GG + HW])
    y = (_dot(yr_ref[...].astype(bf16), wout_ref[0:LRU_W, :])
         + _dot(y_m.astype(bf16), wout_ref[LRU_W:LRU_W + HW, :])
         + _dot(y_g.astype(bf16), wout_ref[LRU_W + HW:, :]))
    y_ref[...] = x_ref[...] + y.reshape(sb, t_len, D_MODEL)

    @pl.when(chunk == pl.num_programs(1) - 1)
    def _store_state():
        tro_ref[...] = extr[:, 0:SUBLANES, :]
        tmo_ref[...] = extm[:, 0:SUBLANES, :]

        def store(s, carry):
            for h in range(ML_H):
                c_ref[s, h] = cst_ref[s, 0:ML_D, h * HP:h * HP + ML_D]
                n_col = cst_ref[s, 0:ML_D, h * HP + ONE_LANE:h * HP + ONE_LANE + 1]
                n_ref[s, h:h + 1, :] = jnp.sum(jnp.where(eye_d, n_col, 0.0), axis=0, keepdims=True)
            return carry

        lax.fori_loop(0, sb, store, 0)


def _mix(x3d, states, donors, lw, consts, *, layer, sb, t_len):
    n_seq, length, _ = x3d.shape
    n_rows = sb * t_len
    n_sub = t_len // min(GLA_SUB, t_len)
    grid = (n_seq // sb, length // t_len)

    def seq_spec(arr):
        blk = (None, sb) + arr.shape[2:]
        zeros = (0,) * (arr.ndim - 2)
        return pl.BlockSpec(blk, lambda i, c: (layer, i) + zeros, pipeline_mode=pl.Buffered(1))

    weights = [lw[k] for k in _MIX_WEIGHT_KEYS]
    assert len(states) == N_STATES and len(consts) == N_MIX_CONSTS
    x_spec = pl.BlockSpec((sb, t_len, D_MODEL), lambda i, c: (i, c, 0))
    row_scratch = lambda width, dtype=f32: pltpu.VMEM((n_rows, width), dtype)
    opnd = bf16 if t_len % BF16_SUBLANES == 0 else f32
    outs = pl.pallas_call(
        functools.partial(_mix_body, sb=sb, t_len=t_len, n_alias=len(donors)),
        out_shape=[jax.ShapeDtypeStruct(x3d.shape, f32)] + [jax.ShapeDtypeStruct(a.shape, f32) for a in states],
        grid=grid,
        in_specs=([x_spec] + [seq_spec(a) for a in states] + [_layer_spec(w.shape, layer) for w in weights]
                  + [_const_spec(c.shape) for c in consts] + [pl.BlockSpec(memory_space=pl.ANY) for _ in donors]),
        out_specs=[x_spec] + [seq_spec(a) for a in states],
        input_output_aliases={N_MIX_INPUTS + k: 1 + k for k in range(len(donors))},
        scratch_shapes=[
            pltpu.VMEM((sb, t_len + SUBLANES, LRU_W), f32),
            pltpu.VMEM((sb, t_len + SUBLANES, HW), f32),
            row_scratch(Z_W),
            row_scratch(HW, opnd),
            row_scratch(HW, opnd),
            row_scratch(HW, opnd),
            row_scratch(HW, opnd),
            row_scratch(ML_H * t_len),
            row_scratch(HW),
            row_scratch(HW),
            row_scratch(LANES),
            pltpu.VMEM((sb, 1, HW), f32),
            row_scratch(HW),
            row_scratch(GLA_KP),
            row_scratch(GLA_KP),
            pltpu.VMEM((sb * n_sub, 1, GLA_KP), f32),
            row_scratch(HW),
            row_scratch(LRU_W),
            row_scratch(HW),
            pltpu.VMEM((sb, HP, HW), f32),
        ],
        compiler_params=pltpu.CompilerParams(
            dimension_semantics=("arbitrary", "arbitrary"), vmem_limit_bytes=VMEM_LIMIT_BYTES),
        name="mixer",
    )(x3d, *states, *weights, *consts, *donors)
    return outs[0], list(outs[1:])


def _pad_last(w, width):
    return jnp.pad(w, [(0, 0)] * (w.ndim - 1) + [(0, width - w.shape[-1])])


def _pad_heads(w, d, dp):
    lead = w.shape[:-1]
    w = _pad_last(w.reshape(lead + (w.shape[-1] // d, d)), dp)
    return w.reshape(lead + (-1,))


def _pad_head_rows(w, d, dp):
    return jnp.swapaxes(_pad_heads(jnp.swapaxes(w, -1, -2), d, dp), -1, -2)


def _block_diag(blocks, dp=None):
    depth, n, d, _ = blocks.shape
    dp = d if dp is None else dp
    blocks = jnp.pad(blocks, ((0, 0), (0, 0), (0, dp - d), (0, dp - d)))
    eye = jnp.eye(n, dtype=blocks.dtype)
    return jnp.einsum("lhij,hg->lhigj", blocks, eye).reshape(depth, n * dp, n * dp)


def _vec_rows(v):
    v = v.astype(f32)
    v = v[:, None, :] if v.ndim == 2 else v
    return _pad_last(v, D_MODEL)


def _tail(conv_state, d=None, dp=None):
    conv_state = conv_state.astype(f32)
    if d is not None:
        conv_state = _pad_heads(conv_state, d, dp)
    return jnp.pad(conv_state, ((0, 0), (0, 0), (SUBLANES - (CONV_W - 1), 0), (0, 0)))


def _untail(tail, d=None, dp=None):
    tail = tail[:, :, SUBLANES - (CONV_W - 1):, :]
    if d is not None:
        tail = tail.reshape(tail.shape[:3] + (-1, dp))[..., :d].reshape(tail.shape[:3] + (-1,))
    return tail


def _gla_state_in(s):
    s = jnp.transpose(s.astype(f32), (0, 1, 4, 2, 3))
    s = jnp.pad(s, ((0, 0), (0, 0), (0, HP - GLA_DV), (0, 0), (0, KP - GLA_DK)))
    return s.reshape(s.shape[:2] + (HP, GLA_KP))


def _gla_state_out(st):
    st = st.reshape(st.shape[:2] + (HP, GLA_H, KP))[:, :, :GLA_DV, :, :GLA_DK]
    return jnp.transpose(st, (0, 1, 3, 4, 2))


def _mix_consts(t_len):
    head = jnp.arange(LANES)[:, None]
    lane = jnp.arange(HW)
    e = (head == (lane // HP)[None, :]).astype(bf16)
    e_rt = (head == (jnp.arange(ML_H * t_len) // t_len)[None, :]).astype(bf16)
    real = (lane % HP) < ML_D
    w_ms = (((lane // HP)[:, None] == (lane // HP)[None, :]) & real[:, None]).astype(bf16)
    k_lane = jnp.arange(GLA_KP)
    wred = (((k_lane // KP)[:, None] == (lane // HP)[None, :]) & ((k_lane % KP) < GLA_DK)[:, None]).astype(bf16)
    return wred, jnp.concatenate([e, e], axis=0), jnp.concatenate([e_rt, e_rt], axis=0), w_ms


def kernel(x_prompt, x_sample, state_lru_h, state_lru_conv, state_mlstm_C, state_mlstm_n, state_mlstm_m, state_mlstm_conv, state_gla_S, meta_tokens, ffn1_norm_g, ffn1_w1, ffn1_w3, ffn1_w2, mix_norm_g, w_in, lru_conv_w, lru_conv_b, lru_wa, lru_ba, lru_wx, lru_bx, lru_lambda, ml_conv_w, ml_conv_b, ml_wq, ml_wk, ml_wv, ml_w_if, ml_b_if, ml_norm_g, ml_skip, gla_w_up, gla_b_up, gla_norm_g, w_out, ffn2_norm_g, ffn2_w1, ffn2_w3, ffn2_w2, final_norm_g):
    n_prompt = x_prompt.shape[0]
    n_sample, t_sample, _ = x_sample.shape

    sizes = (LRU_W, LRU_W, ML_W, ML_W, GLA_KW, GLA_KW, GLA_VW, GLA_VW, GLA_RANK)
    pads = (None, None, (ML_D, HP), (ML_D, HP), (GLA_DK, KP), (GLA_DK, KP), (GLA_DV, HP), (GLA_DV, HP), None)
    parts, off = [], 0
    for size, pad in zip(sizes, pads):
        part = w_in[..., off:off + size]
        parts.append(part if pad is None else _pad_heads(part, *pad))
        off += size
    parts[-1] = _pad_last(parts[-1], LANES)
    wif = jnp.concatenate([_pad_head_rows(ml_w_if[:, i * ML_W:(i + 1) * ML_W], ML_D, HP) for i in range(3)],
                          axis=1)
    wif = jnp.concatenate([_pad_last(wif[..., :ML_H], LANES), _pad_last(wif[..., ML_H:], LANES)], axis=-1)
    bif = jnp.concatenate([_pad_last(ml_b_if[:, :ML_H], LANES), _pad_last(ml_b_if[:, ML_H:], LANES)], axis=-1)
    wo = jnp.concatenate([w_out[:, :LRU_W], _pad_head_rows(w_out[:, LRU_W:LRU_W + ML_W], ML_D, HP),
                          _pad_head_rows(w_out[:, LRU_W + ML_W:], GLA_DV, HP)], axis=1)
    ones_row = jnp.broadcast_to(((jnp.arange(HW) % HP) == ONE_LANE).astype(f32), (DEPTH, HW))
    vec_rows = [mix_norm_g, lru_conv_w, lru_conv_b, lru_ba, lru_bx, lru_lambda,
                _pad_heads(ml_conv_w, ML_D, HP), _pad_heads(ml_conv_b, ML_D, HP), bif,
                _pad_heads(ml_norm_g, ML_D, HP), _pad_heads(ml_skip, ML_D, HP),
                _pad_heads(gla_b_up, GLA_DK, KP), _pad_heads(gla_norm_g, GLA_DV, HP), ones_row]
    vecs = jnp.concatenate([_vec_rows(v) for v in vec_rows], axis=1)
    vecs = jnp.pad(vecs, ((0, 0), (0, VEC_ROWS - vecs.shape[1]), (0, 0)))
    lw = dict(
        vecs=vecs,
        w_in=jnp.concatenate(parts, axis=-1).astype(bf16),
        lru_wa=_block_diag(lru_wa).astype(bf16), lru_wx=_block_diag(lru_wx).astype(bf16),
        ml_wq=_block_diag(ml_wq, HP).astype(bf16), ml_wk=_block_diag(ml_wk, HP).astype(bf16),
        ml_wv=_block_diag(ml_wv, HP).astype(bf16), ml_wif=wif.astype(bf16),
        gla_wup=jnp.pad(_pad_heads(gla_w_up, GLA_DK, KP), ((0, 0), (0, LANES - GLA_RANK), (0, 0))).astype(bf16),
        w_out=wo.astype(bf16))
    ffn1 = (ffn1_norm_g.astype(f32)[:, None, :], ffn1_w1.astype(bf16), ffn1_w3.astype(bf16), ffn1_w2.astype(bf16))
    ffn2 = (ffn2_norm_g.astype(f32)[:, None, :], ffn2_w1.astype(bf16), ffn2_w3.astype(bf16), ffn2_w2.astype(bf16))
    final_g = final_norm_g.astype(f32)[None, :]

    def run_trunk(x3d, states, *, sb, t_len, tm, need_output):
        n_seq, length, _ = x3d.shape
        consts = _mix_consts(t_len)
        new_states = ()
        for l in range(DEPTH):
            x2d = _ffn(x3d.reshape(n_seq * length, D_MODEL), *ffn1, final_g, layer=l, tm=tm, post_norm=False)
            x3d, new_states = _mix(x2d.reshape(n_seq, length, D_MODEL), states, new_states, lw, consts,
                                   layer=l, sb=sb, t_len=t_len)
            last = l == DEPTH - 1
            if need_output or not last:
                x2d = _ffn(x3d.reshape(n_seq * length, D_MODEL), *ffn2, final_g, layer=l, tm=tm, post_norm=last)
                x3d = x2d.reshape(n_seq, length, D_MODEL)
        return x3d, new_states

    zeros = lambda *shape: jnp.zeros((DEPTH, n_prompt) + shape, f32)
    zero_states = [zeros(LRU_W), zeros(SUBLANES, LRU_W), zeros(SUBLANES, HW), zeros(ML_H, ML_D, ML_D),
                   zeros(ML_H, ML_D), zeros(1, LANES), zeros(HP, GLA_KP)]
    x_meta = jnp.broadcast_to(meta_tokens.astype(f32)[None], (n_prompt, N_META, D_MODEL))
    _, meta_states = run_trunk(x_meta, zero_states, sb=n_prompt, t_len=N_META, tm=n_prompt * N_META,
                               need_output=False)

    y_prompt, p_states = run_trunk(x_prompt, meta_states, sb=n_prompt, t_len=CHUNK, tm=512, need_output=True)

    s_states = [state_lru_h.astype(f32), _tail(state_lru_conv), _tail(state_mlstm_conv, ML_D, HP),
                state_mlstm_C.astype(f32), state_mlstm_n.astype(f32),
                _pad_last(state_mlstm_m.astype(f32), LANES)[:, :, None, :], _gla_state_in(state_gla_S)]
    y_sample, s_states = run_trunk(x_sample, s_states, sb=16, t_len=t_sample, tm=512, need_output=True)

    def reference_layout(st):
        lru_h, lru_tail, ml_tail, ml_c, ml_n, ml_m, gla_s = st
        return (lru_h, _untail(lru_tail), ml_c, ml_n, ml_m[:, :, 0, :ML_H], _untail(ml_tail, ML_D, HP),
                _gla_state_out(gla_s))

    return (y_prompt, y_sample) + reference_layout(p_states) + reference_layout(s_states)
```

```python
import functools

import jax
import jax.numpy as jnp
from jax import lax
from jax.experimental import pallas as pl
from jax.experimental.pallas import tpu as pltpu

f32 = jnp.float32
bf16 = jnp.bfloat16

D_MODEL = 1024
DEPTH = 2
N_META = 16
CONV_W = 4
CHUNK = 64
LRU_W = 256
LRU_BLOCKS = 4
LRU_C = 8.0
ML_H = 4
ML_D = 96
ML_W = ML_H * ML_D
GLA_H = 4
GLA_DK = 48
GLA_DV = 96
GLA_KW = GLA_H * GLA_DK
GLA_VW = GLA_H * GLA_DV
GLA_RANK = 16
GLA_TAU = 16.0
D_FF = 2816
EPS = 1e-6

SUBLANES = 8
LANES = 128
BF16_SUBLANES = 16
VMEM_LIMIT_BYTES = 60 * 1024 * 1024

FF_TILE = 256
N_FF_TILES = D_FF // FF_TILE
GLA_SUB = 16

HP = LANES
HW = ML_H * HP
ONE_LANE = ML_D
KP = 64
GLA_KP = GLA_H * KP

Z_UR = 0
Z_GR = Z_UR + LRU_W
Z_UM = Z_GR + LRU_W
Z_ZM = Z_UM + HW
Z_QG = Z_ZM + HW
Z_KG = Z_QG + GLA_KP
Z_VG = Z_KG + GLA_KP
Z_GG = Z_VG + HW
Z_AL = Z_GG + HW
Z_W = Z_AL + LANES


def _dot(a, b):
    return jnp.dot(a, b, preferred_element_type=f32)


def _dot_nt(a, b):
    return lax.dot_general(a, b, (((1,), (1,)), ((), ())), preferred_element_type=f32)


def _dot_tn(a, b):
    return lax.dot_general(a, b, (((0,), (0,)), ((), ())), preferred_element_type=f32)


def _rms(x, g):
    return x * lax.rsqrt(jnp.mean(x * x, axis=-1, keepdims=True) + EPS) * g


def _ffn_body(x_ref, g_ref, w1_ref, w3_ref, w2_ref, gf_ref, o_ref, gated_ref, *, post_norm):
    x = x_ref[...]
    h = _rms(x, g_ref[...]).astype(bf16)
    for f in range(N_FF_TILES):
        cols = slice(f * FF_TILE, (f + 1) * FF_TILE)
        a = _dot(h, w1_ref[:, cols])
        b = _dot(h, w3_ref[:, cols])
        gated_ref[:, cols] = (a * jax.nn.sigmoid(a) * b).astype(bf16)
    y = x + 0.5 * _dot(gated_ref[...], w2_ref[...])
    if post_norm:
        y = _rms(y, gf_ref[...])
    o_ref[...] = y


def _const_spec(shape):
    zeros = (0,) * len(shape)
    return pl.BlockSpec(shape, lambda *_: zeros, pipeline_mode=pl.Buffered(1))


def _layer_spec(shape, layer):
    zeros = (0,) * (len(shape) - 1)
    return pl.BlockSpec((None,) + tuple(shape[1:]), lambda *_: (layer,) + zeros, pipeline_mode=pl.Buffered(1))


def _ffn(x2d, g, w1, w3, w2, gf, *, layer, tm, post_norm):
    rows = x2d.shape[0]
    return pl.pallas_call(
        functools.partial(_ffn_body, post_norm=post_norm),
        out_shape=jax.ShapeDtypeStruct(x2d.shape, f32),
        grid=(rows // tm,),
        in_specs=[
            pl.BlockSpec((tm, D_MODEL), lambda i: (i, 0)),
            _layer_spec(g.shape, layer),
            _layer_spec(w1.shape, layer),
            _layer_spec(w3.shape, layer),
            _layer_spec(w2.shape, layer),
            _const_spec(gf.shape),
        ],
        out_specs=pl.BlockSpec((tm, D_MODEL), lambda i: (i, 0)),
        scratch_shapes=[pltpu.VMEM((tm, D_FF), bf16)],
        compiler_params=pltpu.CompilerParams(
            dimension_semantics=("arbitrary",), vmem_limit_bytes=VMEM_LIMIT_BYTES),
        name="ffn",
    )(x2d, g, w1, w3, w2, gf)


def _row_time(shape, period):
    return lax.broadcasted_iota(jnp.int32, shape, 0) % period


def _seg_cumsum(x, period):
    t = _row_time(x.shape, period)
    s = 1
    while s < period:
        x = x + jnp.where(t >= s, pltpu.roll(x, s, 0), 0.0)
        s *= 2
    return x


def _rows_from_seq(v, t_len):
    sb, width = v.shape
    return jnp.broadcast_to(v[:, None, :], (sb, t_len, width)).reshape(sb * t_len, width)


def _last_rows(x, groups, period):
    return x.reshape(groups, period, x.shape[-1])[:, period - 1, :]


def _expand(x, e2):
    hi = x.astype(bf16)
    lo = (x - hi.astype(f32)).astype(bf16)
    return _dot(jnp.concatenate([hi, lo], axis=1), e2)


def _causal_conv(ext_ref, u, w, b, sb, t_len):
    width = u.shape[-1]
    ext_ref[:, SUBLANES:SUBLANES + t_len, :] = u.reshape(sb, t_len, width)
    base = SUBLANES - (CONV_W - 1)
    out = b + ext_ref[:, base:base + t_len, :] * w[0:1, :]
    for j in range(1, CONV_W):
        out = out + ext_ref[:, base + j:base + j + t_len, :] * w[j:j + 1, :]
    ext_ref[:, 0:SUBLANES, :] = ext_ref[:, t_len:t_len + SUBLANES, :]
    return out.reshape(sb * t_len, width)


V_MIX_G = 0
V_LRU_CW = 1
V_LRU_CB = 5
V_LRU_BA = 6
V_LRU_BX = 7
V_LRU_LAM = 8
V_ML_CW = 9
V_ML_CB = 13
V_ML_BIF = 14
V_ML_G = 15
V_ML_SKIP = 16
V_GLA_BUP = 17
V_GLA_G = 18
V_ONES = 19
VEC_ROWS = 24
N_STATES = 7
_MIX_WEIGHT_KEYS = ("vecs", "w_in", "lru_wa", "lru_wx", "ml_wq", "ml_wk", "ml_wv", "ml_wif", "gla_wup", "w_out")
N_MIX_CONSTS = 3
N_MIX_INPUTS = 1 + N_STATES + len(_MIX_WEIGHT_KEYS) + N_MIX_CONSTS


def _mix_body(*refs, sb, t_len, n_alias):
    (x_ref, h0_ref, tr0_ref, tm0_ref, c0_ref, n0_ref, m0_ref, s0_ref,
     vec_ref, win_ref, wa_ref, wx_ref, wq_ref, wk_ref, wv_ref, wif_ref, wup_ref, wout_ref,
     e2_ref, ert2_ref, wms_ref) = refs[:N_MIX_INPUTS]
    (y_ref, h_ref, tro_ref, tmo_ref, c_ref, n_ref, m_ref, s_ref,
     extr, extm, z_ref, mq_ref, mk_ref, kw_ref, va_ref, rtx_ref, wix_ref, eix_ref, cc_ref,
     decx_ref, hm_ref, qt_ref, kh_ref, gdec_ref, abig_ref, og_ref, yr_ref, cm_ref,
     cst_ref) = refs[N_MIX_INPUTS + n_alias:]

    def vec(row, width, n=1):
        return vec_ref[row:row + n, 0:width]

    chunk = pl.program_id(1)
    n_rows = sb * t_len
    eye_d = (lax.broadcasted_iota(jnp.int32, (ML_D, ML_D), 0)
             == lax.broadcasted_iota(jnp.int32, (ML_D, ML_D), 1))

    @pl.when(chunk == 0)
    def _load_state():
        h_ref[...] = h0_ref[...]
        extr[:, 0:SUBLANES, :] = tr0_ref[...]
        extm[:, 0:SUBLANES, :] = tm0_ref[...]
        m_ref[...] = m0_ref[...]
        s_ref[...] = s0_ref[...]

        def load(s, carry):
            cst_ref[s] = jnp.zeros((HP, HW), f32)
            for h in range(ML_H):
                cst_ref[s, 0:ML_D, h * HP:h * HP + ML_D] = c0_ref[s, h]
                n_col = jnp.sum(jnp.where(eye_d, n0_ref[s, h:h + 1, :], 0.0), axis=1, keepdims=True)
                cst_ref[s, 0:ML_D, h * HP + ONE_LANE:h * HP + ONE_LANE + 1] = n_col
            return carry

        lax.fori_loop(0, sb, load, 0)

    x = x_ref[...].reshape(n_rows, D_MODEL)
    hn = _rms(x, vec(V_MIX_G, D_MODEL)).astype(bf16)
    z_ref[...] = _dot(hn, win_ref[...])

    xr = _causal_conv(extr, z_ref[:, Z_UR:Z_UR + LRU_W], vec(V_LRU_CW, LRU_W, CONV_W), vec(V_LRU_CB, LRU_W),
                      sb, t_len)
    xr_b = xr.astype(bf16)
    r = jax.nn.sigmoid(_dot(xr_b, wa_ref[...]) + vec(V_LRU_BA, LRU_W))
    ig = jax.nn.sigmoid(_dot(xr_b, wx_ref[...]) + vec(V_LRU_BX, LRU_W))
    log_a = -LRU_C * r * jax.nn.softplus(-vec(V_LRU_LAM, LRU_W))
    a = jnp.exp(log_a)
    bt = jnp.sqrt(1.0 - a * a) * (ig * xr)
    tl = _row_time((n_rows, LRU_W), t_len)
    s = 1
    while s < t_len:
        keep = tl >= s
        a_prev = jnp.where(keep, pltpu.roll(a, s, 0), 1.0)
        b_prev = jnp.where(keep, pltpu.roll(bt, s, 0), 0.0)
        bt = a * b_prev + bt
        a = a * a_prev
        s *= 2
    hs = bt + a * _rows_from_seq(h_ref[...], t_len)
    h_ref[...] = _last_rows(hs, sb, t_len)
    yr_ref[...] = jax.nn.gelu(z_ref[:, Z_GR:Z_GR + LRU_W]) * hs

    u_m = z_ref[:, Z_UM:Z_UM + HW]
    cm = jax.nn.silu(_causal_conv(extm, u_m, vec(V_ML_CW, HW, CONV_W), vec(V_ML_CB, HW), sb, t_len))
    cm_ref[...] = cm
    cm_b = cm.astype(bf16)
    mq = _dot(cm_b, wq_ref[...])
    mk = _dot(cm_b, wk_ref[...])
    va = _dot(u_m.astype(bf16), wv_ref[...]) + vec(V_ONES, HW)
    gates = _dot(jnp.concatenate([mq.astype(bf16), mk.astype(bf16), va.astype(bf16)], axis=1),
                 wif_ref[...]) + vec(V_ML_BIF, 2 * LANES)
    li = gates[:, 0:LANES]
    lf = jax.nn.log_sigmoid(gates[:, LANES:2 * LANES])
    tg = _row_time((n_rows, LANES), t_len)
    bcum, m_loc = lf, li
    s = 1
    while s < t_len:
        keep = tg >= s
        b_prev = jnp.where(keep, pltpu.roll(bcum, s, 0), 0.0)
        m_prev_seg = jnp.where(keep, pltpu.roll(m_loc, s, 0), -jnp.inf)
        m_loc = jnp.maximum(m_prev_seg + bcum, m_loc)
        bcum = bcum + b_prev
        s *= 2
    m_old = m_ref[:, 0, :]
    m_old_rows = _rows_from_seq(m_old, t_len)
    m_t = jnp.maximum(bcum + m_old_rows, m_loc)
    m_new = _last_rows(m_t, sb, t_len)
    b_last = _last_rows(bcum, sb, t_len)
    m_ref[...] = m_new[:, None, :]
    cc = li - bcum
    cc_ref[...] = cc
    e2 = e2_ref[...]
    rtx_ref[...] = _expand(bcum - m_t, ert2_ref[...])
    wix_ref[...] = jnp.exp(_expand(bcum + m_old_rows - m_t, e2))
    eix_ref[...] = jnp.exp(_expand(-m_t, e2))
    w_k = jnp.exp(_expand(_rows_from_seq(b_last, t_len) + cc - _rows_from_seq(m_new, t_len), e2))
    decx_ref[...] = jnp.exp(_expand(b_last + m_old - m_new, e2))[:, None, :]
    mq_ref[...] = (mq * (ML_D ** -0.5)).astype(mq_ref.dtype)
    mk_ref[...] = mk.astype(mk_ref.dtype)
    kw_ref[...] = (mk * w_k).astype(kw_ref.dtype)
    va_ref[...] = va.astype(va_ref.dtype)

    t4 = ML_H * t_len
    causal_cat = (lax.broadcasted_iota(jnp.int32, (t_len, t4), 0)
                  >= lax.broadcasted_iota(jnp.int32, (t_len, t4), 1) % t_len)
    stack_diag = (lax.broadcasted_iota(jnp.int32, (t4, HW), 0) // t_len
                  == lax.broadcasted_iota(jnp.int32, (t4, HW), 1) // HP)
    state_diag = (lax.broadcasted_iota(jnp.int32, (HW, HW), 0) // HP
                  == lax.broadcasted_iota(jnp.int32, (HW, HW), 1) // HP)
    lane_head = lax.broadcasted_iota(jnp.int32, (t_len, HW), 1) // HP

    def mlstm_seq(s, carry):
        rows = pl.ds(pl.multiple_of(s * t_len, t_len), t_len)
        q_b = mq_ref[rows, :].astype(bf16)
        k_bd = jnp.where(stack_diag, jnp.concatenate([mk_ref[rows, :]] * ML_H, axis=0), 0.0).astype(bf16)
        v_bd = jnp.where(stack_diag, jnp.concatenate([va_ref[rows, :]] * ML_H, axis=0), 0.0).astype(bf16)
        c_t = cc_ref[rows, :].T
        c_row = jnp.concatenate([c_t[h:h + 1, :] for h in range(ML_H)], axis=1)
        dmat = jnp.where(causal_cat, rtx_ref[rows, :] + c_row, -jnp.inf)
        sc = _dot_nt(q_b, k_bd) * jnp.exp(dmat)
        c_old = cst_ref[s]
        c_bd = jnp.where(state_diag, jnp.concatenate([c_old] * ML_H, axis=0), 0.0).astype(bf16)
        num = _dot(sc.astype(bf16), v_bd) + _dot(q_b, c_bd) * wix_ref[rows, :]
        den = jnp.zeros((t_len, HW), f32)
        for h in range(ML_H):
            den = jnp.where(lane_head == h, num[:, h * HP + ONE_LANE:h * HP + ONE_LANE + 1], den)
        hm_ref[rows, :] = num / jnp.maximum(jnp.abs(den), eix_ref[rows, :])
        kw_st = jnp.concatenate([kw_ref[rows, h * HP:(h + 1) * HP] for h in range(ML_H)],
                                axis=0).astype(bf16)
        cst_ref[s] = c_old * decx_ref[s] + _dot_tn(kw_st, v_bd)
        return carry

    lax.fori_loop(0, sb, mlstm_seq, 0, unroll=2)

    sub = min(GLA_SUB, t_len)
    n_sub = t_len // sub
    al_b = z_ref[:, Z_AL:Z_AL + LANES].astype(bf16)
    lg = jax.nn.log_sigmoid(_dot(al_b, wup_ref[...]) + vec(V_GLA_BUP, GLA_KP)) * (1.0 / GLA_TAU)
    bcl = _seg_cumsum(lg, sub)
    gq = z_ref[:, Z_QG:Z_QG + GLA_KP] * (GLA_DK ** -0.5)
    gk = z_ref[:, Z_KG:Z_KG + GLA_KP]
    last =_last_rows(bcl, n_rows // sub, sub)
    gdec_ref[...] = jnp.exp(last)[:, None, :]
    qt_ref[...] = gq * jnp.exp(bcl)
    kh_ref[...] = gk * jnp.exp(_rows_from_seq(last, sub) - bcl)
    n_grp = n_rows // sub

    def grp_row(arr, i):
        picked = arr.reshape(n_grp, sub, arr.shape[-1])[:, i, :]
        return jnp.broadcast_to(picked[:, None, :], (n_grp, sub, arr.shape[-1])).reshape(arr.shape)

    ts = _row_time((n_rows, GLA_KP), sub)
    key_head_r = lax.broadcasted_iota(jnp.int32, (GLA_KP, LANES), 0) // KP
    out_lane = lax.broadcasted_iota(jnp.int32, (GLA_KP, LANES), 1)
    out_src = jnp.where(key_head_r == out_lane // sub, out_lane % sub, -1)
    a_cat = jnp.zeros((n_rows, LANES), f32)
    for i in range(sub):
        e = jnp.where(ts >= i, gq * grp_row(gk, i) * jnp.exp(bcl - grp_row(bcl, i)), 0.0)
        a_cat = a_cat + _dot(e.astype(bf16), jnp.where(out_src == i, 1.0, 0.0).astype(bf16))
    ex_r = lax.broadcasted_iota(jnp.int32, (LANES, t4), 0)
    ex_l = lax.broadcasted_iota(jnp.int32, (LANES, t4), 1)
    ex = jnp.where(ex_r // sub == ex_l // t_len, jnp.where(ex_r % sub == ex_l % sub, 1.0, 0.0), 0.0).astype(bf16)
    row_grp = (lax.broadcasted_iota(jnp.int32, (n_rows, t4), 0) % t_len) // sub
    lane_grp = (lax.broadcasted_iota(jnp.int32, (n_rows, t4), 1) % t_len) // sub
    abig_ref[...] = jnp.where(row_grp == lane_grp, _dot(a_cat.astype(bf16), ex), 0.0)

    stack_head = lax.broadcasted_iota(jnp.int32, (GLA_H * sub, GLA_KP), 0) // sub
    key_head = lax.broadcasted_iota(jnp.int32, (GLA_H * sub, GLA_KP), 1) // KP
    head_diag = stack_head == key_head

    def gla_seq(s, carry):
        st = s_ref[s]
        rows_t = pl.ds(pl.multiple_of(s * t_len, t_len), t_len)
        gv_bd = jnp.where(stack_diag, jnp.concatenate([z_ref[rows_t, Z_VG:Z_VG + HW]] * GLA_H, axis=0),
                          0.0).astype(bf16)
        og_ref[rows_t, :] = _dot(abig_ref[rows_t, :].astype(bf16), gv_bd)
        for j in range(n_sub):
            rows = pl.ds(pl.multiple_of(s * t_len + j * sub, sub), sub)
            q_bd = jnp.where(head_diag, jnp.concatenate([qt_ref[rows, :]] * GLA_H, axis=0), 0.0).astype(bf16)
            k_bd = jnp.where(head_diag, jnp.concatenate([kh_ref[rows, :]] * GLA_H, axis=0), 0.0).astype(bf16)
            v_st = jnp.concatenate([z_ref[rows, Z_VG + h * HP:Z_VG + (h + 1) * HP] for h in range(GLA_H)],
                                   axis=0).astype(bf16)
            o4 = _dot_nt(q_bd, st.astype(bf16))
            for h in range(GLA_H):
                lanes = slice(h * HP, (h + 1) * HP)
                og_ref[rows, lanes] = og_ref[rows, lanes] + o4[h * sub:(h + 1) * sub, :]
            st = st * gdec_ref[s * n_sub + j] + _dot_tn(v_st, k_bd)
        s_ref[s] = st
        return carry

    lax.fori_loop(0, sb, gla_seq, 0, unroll=2)

    def head_norm(o):
        ms = _dot((o * o).astype(bf16), wms_ref[...]) * (1.0 / ML_D)
        return o * lax.rsqrt(ms + EPS)

    y_m = (jax.nn.sigmoid(z_ref[:, Z_ZM:Z_ZM + HW])
           * (head_norm(hm_ref[...]) * vec(V_ML_G, HW) + vec(V_ML_SKIP, HW) * cm_ref[...]))
    y_g = head_norm(og_ref[...]) * vec(V_GLA_G, HW) * jax.nn.silu(z_ref[:, Z_GG:Z_GG + HW])
    y = (_dot(yr_ref[...].astype(bf16), wout_ref[0:LRU_W, :])
         + _dot(y_m.astype(bf16), wout_ref[LRU_W:LRU_W + HW, :])
         + _dot(y_g.astype(bf16), wout_ref[LRU_W + HW:, :]))
    y_ref[...] = x_ref[...] + y.reshape(sb, t_len, D_MODEL)

    @pl.when(chunk == pl.num_programs(1) - 1)
    def _store_state():
        tro_ref[...] = extr[:, 0:SUBLANES, :]
        tmo_ref[...] = extm[:, 0:SUBLANES, :]

        def store(s, carry):
            for h in range(ML_H):
                c_ref[s, h] = cst_ref[s, 0:ML_D, h * HP:h * HP + ML_D]
                n_col = cst_ref[s, 0:ML_D, h * HP + ONE_LANE:h * HP + ONE_LANE + 1]
                n_ref[s, h:h + 1, :] = jnp.sum(jnp.where(eye_d, n_col, 0.0), axis=0, keepdims=True)
            return carry

        lax.fori_loop(0, sb, store, 0)


def _mix(x3d, states, donors, lw, consts, *, layer, sb, t_len):
    n_seq, length, _ = x3d.shape
    n_rows = sb * t_len
    n_sub = t_len // min(GLA_SUB, t_len)
    grid = (n_seq // sb, length // t_len)

    def seq_spec(arr):
        blk = (None, sb) + arr.shape[2:]
        zeros = (0,) * (arr.ndim - 2)
        return pl.BlockSpec(blk, lambda i, c: (layer, i) + zeros, pipeline_mode=pl.Buffered(1))

    weights = [lw[k] for k in _MIX_WEIGHT_KEYS]
    assert len(states) == N_STATES and len(consts) == N_MIX_CONSTS
    x_spec = pl.BlockSpec((sb, t_len, D_MODEL), lambda i, c: (i, c, 0))
    row_scratch = lambda width, dtype=f32: pltpu.VMEM((n_rows, width), dtype)
    opnd = bf16 if t_len % BF16_SUBLANES == 0 else f32
    outs = pl.pallas_call(
        functools.partial(_mix_body, sb=sb, t_len=t_len, n_alias=len(donors)),
        out_shape=[jax.ShapeDtypeStruct(x3d.shape, f32)] + [jax.ShapeDtypeStruct(a.shape, f32) for a in states],
        grid=grid,
        in_specs=([x_spec] + [seq_spec(a) for a in states] + [_layer_spec(w.shape, layer) for w in weights]
                  + [_const_spec(c.shape) for c in consts] + [pl.BlockSpec(memory_space=pl.ANY) for _ in donors]),
        out_specs=[x_spec] + [seq_spec(a) for a in states],
        input_output_aliases={N_MIX_INPUTS + k: 1 + k for k in range(len(donors))},
        scratch_shapes=[
            pltpu.VMEM((sb, t_len + SUBLANES, LRU_W), f32),
            pltpu.VMEM((sb, t_len + SUBLANES, HW), f32),
            row_scratch(Z_W),
            row_scratch(HW, opnd),
            row_scratch(HW, opnd),
            row_scratch(HW, opnd),
            row_scratch(HW, opnd),
            row_scratch(ML_H * t_len),
            row_scratch(HW),
            row_scratch(HW),
            row_scratch(LANES),
            pltpu.VMEM((sb, 1, HW), f32),
            row_scratch(HW),
            row_scratch(GLA_KP),
            row_scratch(GLA_KP),
            pltpu.VMEM((sb * n_sub, 1, GLA_KP), f32),
            row_scratch(ML_H * t_len),
            row_scratch(HW),
            row_scratch(LRU_W),
            row_scratch(HW),
            pltpu.VMEM((sb, HP, HW), f32),
        ],
        compiler_params=pltpu.CompilerParams(
            dimension_semantics=("arbitrary", "arbitrary"), vmem_limit_bytes=VMEM_LIMIT_BYTES),
        name="mixer",
    )(x3d, *states, *weights, *consts, *donors)
    return outs[0], list(outs[1:])


def _pad_last(w, width):
    return jnp.pad(w, [(0, 0)] * (w.ndim - 1) + [(0, width - w.shape[-1])])


def _pad_heads(w, d, dp):
    lead = w.shape[:-1]
    w = _pad_last(w.reshape(lead + (w.shape[-1] // d, d)), dp)
    return w.reshape(lead + (-1,))


def _pad_head_rows(w, d, dp):
    return jnp.swapaxes(_pad_heads(jnp.swapaxes(w, -1, -2), d, dp), -1, -2)


def _block_diag(blocks, dp=None):
    depth, n, d, _ = blocks.shape
    dp = d if dp is None else dp
    blocks = jnp.pad(blocks, ((0, 0), (0, 0), (0, dp - d), (0, dp - d)))
    eye = jnp.eye(n, dtype=blocks.dtype)
    return jnp.einsum("lhij,hg->lhigj", blocks, eye).reshape(depth, n * dp, n * dp)


def _vec_rows(v):
    v = v.astype(f32)
    v = v[:, None, :] if v.ndim == 2 else v
    return _pad_last(v, D_MODEL)


def _tail(conv_state, d=None, dp=None):
    conv_state = conv_state.astype(f32)
    if d is not None:
        conv_state = _pad_heads(conv_state, d, dp)
    return jnp.pad(conv_state, ((0, 0), (0, 0), (SUBLANES - (CONV_W - 1), 0), (0, 0)))


def _untail(tail, d=None, dp=None):
    tail = tail[:, :, SUBLANES - (CONV_W - 1):, :]
    if d is not None:
        tail = tail.reshape(tail.shape[:3] + (-1, dp))[..., :d].reshape(tail.shape[:3] + (-1,))
    return tail


def _gla_state_in(s):
    s = jnp.transpose(s.astype(f32), (0, 1, 4, 2, 3))
    s = jnp.pad(s, ((0, 0), (0, 0), (0, HP - GLA_DV), (0, 0), (0, KP - GLA_DK)))
    return s.reshape(s.shape[:2] + (HP, GLA_KP))


def _gla_state_out(st):
    st = st.reshape(st.shape[:2] + (HP, GLA_H, KP))[:, :, :GLA_DV, :, :GLA_DK]
    return jnp.transpose(st, (0, 1, 3, 4, 2))


def _mix_consts(t_len):
    head = jnp.arange(LANES)[:, None]
    lane = jnp.arange(HW)
    e = (head == (lane // HP)[None, :]).astype(bf16)
    e_rt = (head == (jnp.arange(ML_H * t_len) // t_len)[None, :]).astype(bf16)
    real = (lane % HP) < ML_D
    w_ms = (((lane // HP)[:, None] == (lane // HP)[None, :]) & real[:, None]).astype(bf16)
    return jnp.concatenate([e, e], axis=0), jnp.concatenate([e_rt, e_rt], axis=0), w_ms


def kernel(x_prompt, x_sample, state_lru_h, state_lru_conv, state_mlstm_C, state_mlstm_n, state_mlstm_m, state_mlstm_conv, state_gla_S, meta_tokens, ffn1_norm_g, ffn1_w1, ffn1_w3, ffn1_w2, mix_norm_g, w_in, lru_conv_w, lru_conv_b, lru_wa, lru_ba, lru_wx, lru_bx, lru_lambda, ml_conv_w, ml_conv_b, ml_wq, ml_wk, ml_wv, ml_w_if, ml_b_if, ml_norm_g, ml_skip, gla_w_up, gla_b_up, gla_norm_g, w_out, ffn2_norm_g, ffn2_w1, ffn2_w3, ffn2_w2, final_norm_g):
    n_prompt = x_prompt.shape[0]
    n_sample, t_sample, _ = x_sample.shape

    sizes = (LRU_W, LRU_W, ML_W, ML_W, GLA_KW, GLA_KW, GLA_VW, GLA_VW, GLA_RANK)
    pads = (None, None, (ML_D, HP), (ML_D, HP), (GLA_DK, KP), (GLA_DK, KP), (GLA_DV, HP), (GLA_DV, HP), None)
    parts, off = [], 0
    for size, pad in zip(sizes, pads):
        part = w_in[..., off:off + size]
        parts.append(part if pad is None else _pad_heads(part, *pad))
        off += size
    parts[-1] = _pad_last(parts[-1], LANES)
    wif = jnp.concatenate([_pad_head_rows(ml_w_if[:, i * ML_W:(i + 1) * ML_W], ML_D, HP) for i in range(3)],
                          axis=1)
    wif = jnp.concatenate([_pad_last(wif[..., :ML_H], LANES), _pad_last(wif[..., ML_H:], LANES)], axis=-1)
    bif = jnp.concatenate([_pad_last(ml_b_if[:, :ML_H], LANES), _pad_last(ml_b_if[:, ML_H:], LANES)], axis=-1)
    wo = jnp.concatenate([w_out[:, :LRU_W], _pad_head_rows(w_out[:, LRU_W:LRU_W + ML_W], ML_D, HP),
                          _pad_head_rows(w_out[:, LRU_W + ML_W:], GLA_DV, HP)], axis=1)
    ones_row = jnp.broadcast_to(((jnp.arange(HW) % HP) == ONE_LANE).astype(f32), (DEPTH, HW))
    vec_rows = [mix_norm_g, lru_conv_w, lru_conv_b, lru_ba, lru_bx, lru_lambda,
                _pad_heads(ml_conv_w, ML_D, HP), _pad_heads(ml_conv_b, ML_D, HP), bif,
                _pad_heads(ml_norm_g, ML_D, HP), _pad_heads(ml_skip, ML_D, HP),
                _pad_heads(gla_b_up, GLA_DK, KP), _pad_heads(gla_norm_g, GLA_DV, HP), ones_row]
    vecs = jnp.concatenate([_vec_rows(v) for v in vec_rows], axis=1)
    vecs = jnp.pad(vecs, ((0, 0), (0, VEC_ROWS - vecs.shape[1]), (0, 0)))
    lw = dict(
        vecs=vecs,
        w_in=jnp.concatenate(parts, axis=-1).astype(bf16),
        lru_wa=_block_diag(lru_wa).astype(bf16), lru_wx=_block_diag(lru_wx).astype(bf16),
        ml_wq=_block_diag(ml_wq, HP).astype(bf16), ml_wk=_block_diag(ml_wk, HP).astype(bf16),
        ml_wv=_block_diag(ml_wv, HP).astype(bf16), ml_wif=wif.astype(bf16),
        gla_wup=jnp.pad(_pad_heads(gla_w_up, GLA_DK, KP), ((0, 0), (0, LANES - GLA_RANK), (0, 0))).astype(bf16),
        w_out=wo.astype(bf16))
    ffn1 = (ffn1_norm_g.astype(f32)[:, None, :], ffn1_w1.astype(bf16), ffn1_w3.astype(bf16), ffn1_w2.astype(bf16))
    ffn2 = (ffn2_norm_g.astype(f32)[:, None, :], ffn2_w1.astype(bf16), ffn2_w3.astype(bf16), ffn2_w2.astype(bf16))
    final_g = final_norm_g.astype(f32)[None, :]

    def run_trunk(x3d, states, *, sb, t_len, tm, need_output):
        n_seq, length, _ = x3d.shape
        consts = _mix_consts(t_len)
        new_states = ()
        for l in range(DEPTH):
            x2d = _ffn(x3d.reshape(n_seq * length, D_MODEL), *ffn1, final_g, layer=l, tm=tm, post_norm=False)
            x3d, new_states = _mix(x2d.reshape(n_seq, length, D_MODEL), states, new_states, lw, consts,
                                   layer=l, sb=sb, t_len=t_len)
            last = l == DEPTH - 1
            if need_output or not last:
                x2d = _ffn(x3d.reshape(n_seq * length, D_MODEL), *ffn2, final_g, layer=l, tm=tm, post_norm=last)
                x3d = x2d.reshape(n_seq, length, D_MODEL)
        return x3d, new_states

    zeros = lambda *shape: jnp.zeros((DEPTH, n_prompt) + shape, f32)
    zero_states = [zeros(LRU_W), zeros(SUBLANES, LRU_W), zeros(SUBLANES, HW), zeros(ML_H, ML_D, ML_D),
                   zeros(ML_H, ML_D), zeros(1, LANES), zeros(HP, GLA_KP)]
    x_meta = jnp.broadcast_to(meta_tokens.astype(f32)[None], (n_prompt, N_META, D_MODEL))
    _, meta_states = run_trunk(x_meta, zero_states, sb=n_prompt, t_len=N_META, tm=n_prompt * N_META,
                               need_output=False)

    y_prompt, p_states = run_trunk(x_prompt, meta_states, sb=n_prompt, t_len=CHUNK, tm=512, need_output=True)

    s_states = [state_lru_h.astype(f32), _tail(state_lru_conv), _tail(state_mlstm_conv, ML_D, HP),
                state_mlstm_C.astype(f32), state_mlstm_n.astype(f32),
                _pad_last(state_mlstm_m.astype(f32), LANES)[:, :, None, :], _gla_state_in(state_gla_S)]
    y_sample, s_states = run_trunk(x_sample, s_states, sb=16, t_len=t_sample, tm=512, need_output=True)

    def reference_layout(st):
        lru_h, lru_tail, ml_tail, ml_c, ml_n, ml_m, gla_s = st
        return (lru_h, _untail(lru_tail), ml_c, ml_n, ml_m[:, :, 0, :ML_H], _untail(ml_tail, ML_D, HP),
                _gla_state_out(gla_s))

    return (y_prompt, y_sample) + reference_layout(p_states) + reference_layout(s_states)
```

```python
import functools

import jax
import jax.numpy as jnp
from jax import lax
from jax.experimental import pallas as pl
from jax.experimental.pallas import tpu as pltpu

f32 = jnp.float32
bf16 = jnp.bfloat16

D_MODEL = 1024
DEPTH = 2
N_META = 16
CONV_W = 4
CHUNK = 64
LRU_W = 256
LRU_BLOCKS = 4
LRU_C = 8.0
ML_H = 4
ML_D = 96
ML_W = ML_H * ML_D
GLA_H = 4
GLA_DK = 48
GLA_DV = 96
GLA_KW = GLA_H * GLA_DK
GLA_VW = GLA_H * GLA_DV
GLA_RANK = 16
GLA_TAU = 16.0
D_FF = 2816
EPS = 1e-6

SUBLANES = 8
LANES = 128
BF16_SUBLANES = 16
VMEM_LIMIT_BYTES = 60 * 1024 * 1024

FF_TILE = 256
N_FF_TILES = D_FF // FF_TILE
GLA_SUB = 16
SEQ_LOOP_TRIPS = 2

HP = LANES
HW = ML_H * HP
ONE_LANE = ML_D
KP = 64
GLA_KP = GLA_H * KP

Z_UR = 0
Z_GR = Z_UR + LRU_W
Z_UM = Z_GR + LRU_W
Z_ZM = Z_UM + HW
Z_QG = Z_ZM + HW
Z_KG = Z_QG + GLA_KP
Z_VG = Z_KG + GLA_KP
Z_GG = Z_VG + HW
Z_AL = Z_GG + HW
Z_W = Z_AL + LANES


def _dot(a, b):
    return jnp.dot(a, b, preferred_element_type=f32)


def _dot_nt(a, b):
    return lax.dot_general(a, b, (((1,), (1,)), ((), ())), preferred_element_type=f32)


def _dot_tn(a, b):
    return lax.dot_general(a, b, (((0,), (0,)), ((), ())), preferred_element_type=f32)


def _rms(x, g):
    return x * lax.rsqrt(jnp.mean(x * x, axis=-1, keepdims=True) + EPS) * g


def _ffn_body(x_ref, g_ref, w1_ref, w3_ref, w2_ref, gf_ref, o_ref, gated_ref, *, post_norm):
    x = x_ref[...]
    h = _rms(x, g_ref[...]).astype(bf16)
    for f in range(N_FF_TILES):
        cols = slice(f * FF_TILE, (f + 1) * FF_TILE)
        a = _dot(h, w1_ref[:, cols])
        b = _dot(h, w3_ref[:, cols])
        gated_ref[:, cols] = (a * jax.nn.sigmoid(a) * b).astype(bf16)
    y = x + 0.5 * _dot(gated_ref[...], w2_ref[...])
    if post_norm:
        y = _rms(y, gf_ref[...])
    o_ref[...] = y


def _const_spec(shape):
    zeros = (0,) * len(shape)
    return pl.BlockSpec(shape, lambda *_: zeros, pipeline_mode=pl.Buffered(1))


def _layer_spec(shape, layer):
    zeros = (0,) * (len(shape) - 1)
    return pl.BlockSpec((None,) + tuple(shape[1:]), lambda *_: (layer,) + zeros, pipeline_mode=pl.Buffered(1))


def _ffn(x2d, g, w1, w3, w2, gf, *, layer, tm, post_norm):
    rows = x2d.shape[0]
    return pl.pallas_call(
        functools.partial(_ffn_body, post_norm=post_norm),
        out_shape=jax.ShapeDtypeStruct(x2d.shape, f32),
        grid=(rows // tm,),
        in_specs=[
            pl.BlockSpec((tm, D_MODEL), lambda i: (i, 0)),
            _layer_spec(g.shape, layer),
            _layer_spec(w1.shape, layer),
            _layer_spec(w3.shape, layer),
            _layer_spec(w2.shape, layer),
            _const_spec(gf.shape),
        ],
        out_specs=pl.BlockSpec((tm, D_MODEL), lambda i: (i, 0)),
        scratch_shapes=[pltpu.VMEM((tm, D_FF), bf16)],
        compiler_params=pltpu.CompilerParams(
            dimension_semantics=("arbitrary",), vmem_limit_bytes=VMEM_LIMIT_BYTES),
        name="ffn",
    )(x2d, g, w1, w3, w2, gf)


def _row_time(shape, period):
    return lax.broadcasted_iota(jnp.int32, shape, 0) % period


def _seg_cumsum(x, period):
    t = _row_time(x.shape, period)
    s = 1
    while s < period:
        x = x + jnp.where(t >= s, pltpu.roll(x, s, 0), 0.0)
        s *= 2
    return x


def _rows_from_seq(v, t_len):
    sb, width = v.shape
    return jnp.broadcast_to(v[:, None, :], (sb, t_len, width)).reshape(sb * t_len, width)


def _last_rows(x, groups, period):
    return x.reshape(groups, period, x.shape[-1])[:, period - 1, :]


def _expand(x, e2):
    hi = x.astype(bf16)
    lo = (x - hi.astype(f32)).astype(bf16)
    return _dot(jnp.concatenate([hi, lo], axis=1), e2)


def _causal_conv(ext_ref, u, w, b, sb, t_len):
    width = u.shape[-1]
    ext_ref[:, SUBLANES:SUBLANES + t_len, :] = u.reshape(sb, t_len, width)
    base = SUBLANES - (CONV_W - 1)
    out = b + ext_ref[:, base:base + t_len, :] * w[0:1, :]
    for j in range(1, CONV_W):
        out = out + ext_ref[:, base + j:base + j + t_len, :] * w[j:j + 1, :]
    ext_ref[:, 0:SUBLANES, :] = ext_ref[:, t_len:t_len + SUBLANES, :]
    return out.reshape(sb * t_len, width)


V_MIX_G = 0
V_LRU_CW = 1
V_LRU_CB = 5
V_LRU_BA = 6
V_LRU_BX = 7
V_LRU_LAM = 8
V_ML_CW = 9
V_ML_CB = 13
V_ML_BIF = 14
V_ML_G = 15
V_ML_SKIP = 16
V_GLA_BUP = 17
V_GLA_G = 18
V_ONES = 19
VEC_ROWS = 24
N_STATES = 7
_MIX_WEIGHT_KEYS = ("vecs", "w_in", "lru_wa", "lru_wx", "ml_wq", "ml_wk", "ml_wv", "ml_wif", "gla_wup", "w_out")
N_MIX_CONSTS = 2
N_MIX_INPUTS = 1 + N_STATES + len(_MIX_WEIGHT_KEYS) + N_MIX_CONSTS


def _mix_body(*refs, sb, t_len, n_alias):
    (x_ref, h0_ref, tr0_ref, tm0_ref, c0_ref, n0_ref, m0_ref, s0_ref,
     vec_ref, win_ref, wa_ref, wx_ref, wq_ref, wk_ref, wv_ref, wif_ref, wup_ref, wout_ref,
     e2_ref, ert2_ref) = refs[:N_MIX_INPUTS]
    (y_ref, h_ref, tro_ref, tmo_ref, c_ref, n_ref, m_ref, s_ref,
     extr, extm, z_ref, mq_ref, mk_ref, kw_ref, va_ref, rtx_ref, wix_ref, eix_ref, cc_ref,
     decx_ref, hm_ref, qt_ref, kh_ref, gdec_ref, abig_ref, og_ref, yr_ref, cm_ref,
     cst_ref) = refs[N_MIX_INPUTS + n_alias:]

    def vec(row, width, n=1):
        return vec_ref[row:row + n, 0:width]

    chunk = pl.program_id(1)
    n_rows = sb * t_len
    eye_d = (lax.broadcasted_iota(jnp.int32, (ML_D, ML_D), 0)
             == lax.broadcasted_iota(jnp.int32, (ML_D, ML_D), 1))

    @pl.when(chunk == 0)
    def _load_state():
        h_ref[...] = h0_ref[...]
        extr[:, 0:SUBLANES, :] = tr0_ref[...]
        extm[:, 0:SUBLANES, :] = tm0_ref[...]
        m_ref[...] = m0_ref[...]
        s_ref[...] = s0_ref[...]

        def load(s, carry):
            cst_ref[s] = jnp.zeros((HP, HW), f32)
            for h in range(ML_H):
                cst_ref[s, 0:ML_D, h * HP:h * HP + ML_D] = c0_ref[s, h]
                n_col = jnp.sum(jnp.where(eye_d, n0_ref[s, h:h + 1, :], 0.0), axis=1, keepdims=True)
                cst_ref[s, 0:ML_D, h * HP + ONE_LANE:h * HP + ONE_LANE + 1] = n_col
            return carry

        lax.fori_loop(0, sb, load, 0)

    x = x_ref[...].reshape(n_rows, D_MODEL)
    hn = _rms(x, vec(V_MIX_G, D_MODEL)).astype(bf16)
    z_ref[...] = _dot(hn, win_ref[...])

    xr = _causal_conv(extr, z_ref[:, Z_UR:Z_UR + LRU_W], vec(V_LRU_CW, LRU_W, CONV_W), vec(V_LRU_CB, LRU_W),
                      sb, t_len)
    xr_b = xr.astype(bf16)
    r = jax.nn.sigmoid(_dot(xr_b, wa_ref[...]) + vec(V_LRU_BA, LRU_W))
    ig = jax.nn.sigmoid(_dot(xr_b, wx_ref[...]) + vec(V_LRU_BX, LRU_W))
    log_a = -LRU_C * r * jax.nn.softplus(-vec(V_LRU_LAM, LRU_W))
    a = jnp.exp(log_a)
    one_m_a2 = 1.0 - a * a
    root = jnp.where(one_m_a2 > 0.0, one_m_a2 * lax.rsqrt(one_m_a2), 0.0)
    bt = root * (ig * xr)
    tl = _row_time((n_rows, LRU_W), t_len)
    s = 1
    while s < t_len:
        keep = tl >= s
        a_prev = jnp.where(keep, pltpu.roll(a, s, 0), 1.0)
        b_prev = jnp.where(keep, pltpu.roll(bt, s, 0), 0.0)
        bt = a * b_prev + bt
        a = a * a_prev
        s *= 2
    hs = bt + a * _rows_from_seq(h_ref[...], t_len)
    h_ref[...] = _last_rows(hs, sb, t_len)
    yr_ref[...] = jax.nn.gelu(z_ref[:, Z_GR:Z_GR + LRU_W]) * hs

    u_m = z_ref[:, Z_UM:Z_UM + HW]
    cm = jax.nn.silu(_causal_conv(extm, u_m, vec(V_ML_CW, HW, CONV_W), vec(V_ML_CB, HW), sb, t_len))
    cm_ref[...] = cm
    cm_b = cm.astype(bf16)
    mq = _dot(cm_b, wq_ref[...])
    mk = _dot(cm_b, wk_ref[...])
    va = _dot(u_m.astype(bf16), wv_ref[...]) + vec(V_ONES, HW)
    gates = _dot(jnp.concatenate([mq.astype(bf16), mk.astype(bf16), va.astype(bf16)], axis=1),
                 wif_ref[...]) + vec(V_ML_BIF, 2 * LANES)
    li = gates[:, 0:LANES]
    lf = jax.nn.log_sigmoid(gates[:, LANES:2 * LANES])
    tg = _row_time((n_rows, LANES), t_len)
    bcum, m_loc = lf, li
    s = 1
    while s < t_len:
        keep = tg >= s
        b_prev = jnp.where(keep, pltpu.roll(bcum, s, 0), 0.0)
        m_prev_seg = jnp.where(keep, pltpu.roll(m_loc, s, 0), -jnp.inf)
        m_loc = jnp.maximum(m_prev_seg + bcum, m_loc)
        bcum = bcum + b_prev
        s *= 2
    m_old = m_ref[:, 0, :]
    m_old_rows = _rows_from_seq(m_old, t_len)
    m_t = jnp.maximum(bcum + m_old_rows, m_loc)
    m_new = _last_rows(m_t, sb, t_len)
    b_last = _last_rows(bcum, sb, t_len)
    m_ref[...] = m_new[:, None, :]
    cc = li - bcum
    cc_ref[...] = cc
    e2 = e2_ref[...]
    rtx_ref[...] = _expand(bcum - m_t, ert2_ref[...])
    wix_ref[...] = jnp.exp(_expand(bcum + m_old_rows - m_t, e2))
    eix_ref[...] = jnp.exp(_expand(-m_t, e2))
    w_k = jnp.exp(_expand(_rows_from_seq(b_last, t_len) + cc - _rows_from_seq(m_new, t_len), e2))
    decx_ref[...] = jnp.exp(_expand(b_last + m_old - m_new, e2))[:, None, :]
    mq_ref[...] = (mq * (ML_D ** -0.5)).astype(mq_ref.dtype)
    mk_ref[...] = mk.astype(mk_ref.dtype)
    kw_ref[...] = (mk * w_k).astype(kw_ref.dtype)
    va_ref[...] = va.astype(va_ref.dtype)

    t4 = ML_H * t_len
    causal_cat = (lax.broadcasted_iota(jnp.int32, (t_len, t4), 0)
                  >= lax.broadcasted_iota(jnp.int32, (t_len, t4), 1) % t_len)
    stack_diag = (lax.broadcasted_iota(jnp.int32, (t4, HW), 0) // t_len
                  == lax.broadcasted_iota(jnp.int32, (t4, HW), 1) // HP)
    state_diag = (lax.broadcasted_iota(jnp.int32, (HW, HW), 0) // HP
                  == lax.broadcasted_iota(jnp.int32, (HW, HW), 1) // HP)
    lane_head = lax.broadcasted_iota(jnp.int32, (t_len, HW), 1) // HP

    def mlstm_seq(s, carry):
        rows = pl.ds(pl.multiple_of(s * t_len, t_len), t_len)
        q_b = mq_ref[rows, :].astype(bf16)
        k_bd = jnp.where(stack_diag, jnp.concatenate([mk_ref[rows, :]] * ML_H, axis=0), 0.0).astype(bf16)
        v_bd = jnp.where(stack_diag, jnp.concatenate([va_ref[rows, :]] * ML_H, axis=0), 0.0).astype(bf16)
        c_t = cc_ref[rows, :].T
        c_row = jnp.concatenate([c_t[h:h + 1, :] for h in range(ML_H)], axis=1)
        dmat = jnp.where(causal_cat, rtx_ref[rows, :] + c_row, -jnp.inf)
        sc = _dot_nt(q_b, k_bd) * jnp.exp(dmat)
        c_old = cst_ref[s]
        c_bd = jnp.where(state_diag, jnp.concatenate([c_old.astype(bf16)] * ML_H, axis=0), 0.0)
        num = _dot(sc.astype(bf16), v_bd) + _dot(q_b, c_bd) * wix_ref[rows, :]
        den = jnp.zeros((t_len, HW), f32)
        for h in range(ML_H):
            den = jnp.where(lane_head == h, num[:, h * HP + ONE_LANE:h * HP + ONE_LANE + 1], den)
        hm_ref[rows, :] = num / jnp.maximum(jnp.abs(den), eix_ref[rows, :])
        kw_st = jnp.concatenate([kw_ref[rows, h * HP:(h + 1) * HP] for h in range(ML_H)],
                                axis=0).astype(bf16)
        cst_ref[s] = c_old * decx_ref[s] + _dot_tn(kw_st, v_bd)
        return carry

    lax.fori_loop(0, sb, mlstm_seq, 0, unroll=sb // SEQ_LOOP_TRIPS)

    sub = min(GLA_SUB, t_len)
    n_sub = t_len // sub
    al_b = z_ref[:, Z_AL:Z_AL + LANES].astype(bf16)
    lg = jax.nn.log_sigmoid(_dot(al_b, wup_ref[...]) + vec(V_GLA_BUP, GLA_KP)) * (1.0 / GLA_TAU)
    bcl = _seg_cumsum(lg, sub)
    gq = z_ref[:, Z_QG:Z_QG + GLA_KP] * (GLA_DK ** -0.5)
    gk = z_ref[:, Z_KG:Z_KG + GLA_KP]
    last =_last_rows(bcl, n_rows // sub, sub)
    gdec_ref[...] = jnp.exp(last)[:, None, :]
    qt_ref[...] = gq * jnp.exp(bcl)
    kh_ref[...] = gk * jnp.exp(_rows_from_seq(last, sub) - bcl)
    n_grp = n_rows // sub

    def grp_row(arr, i):
        picked = arr.reshape(n_grp, sub, arr.shape[-1])[:, i, :]
        return jnp.broadcast_to(picked[:, None, :], (n_grp, sub, arr.shape[-1])).reshape(arr.shape)

    ts = _row_time((n_rows, GLA_KP), sub)
    key_head_r = lax.broadcasted_iota(jnp.int32, (GLA_KP, LANES), 0) // KP
    out_lane = lax.broadcasted_iota(jnp.int32, (GLA_KP, LANES), 1)
    out_src = jnp.where(key_head_r == out_lane // sub, out_lane % sub, -1)
    a_cat = jnp.zeros((n_rows, LANES), f32)
    for i in range(sub):
        e = jnp.where(ts >= i, gq * grp_row(gk, i) * jnp.exp(bcl - grp_row(bcl, i)), 0.0)
        a_cat = a_cat + _dot(e.astype(bf16), jnp.where(out_src == i, 1.0, 0.0).astype(bf16))
    ex_r = lax.broadcasted_iota(jnp.int32, (LANES, t4), 0)
    ex_l = lax.broadcasted_iota(jnp.int32, (LANES, t4), 1)
    ex = jnp.where(ex_r // sub == ex_l // t_len, jnp.where(ex_r % sub == ex_l % sub, 1.0, 0.0), 0.0).astype(bf16)
    row_grp = (lax.broadcasted_iota(jnp.int32, (n_rows, t4), 0) % t_len) // sub
    lane_grp = (lax.broadcasted_iota(jnp.int32, (n_rows, t4), 1) % t_len) // sub
    abig_ref[...] = jnp.where(row_grp == lane_grp, _dot(a_cat.astype(bf16), ex), 0.0)

    stack_head = lax.broadcasted_iota(jnp.int32, (GLA_H * sub, GLA_KP), 0) // sub
    key_head = lax.broadcasted_iota(jnp.int32, (GLA_H * sub, GLA_KP), 1) // KP
    head_diag = stack_head == key_head

    def gla_seq(s, carry):
        st = s_ref[s]
        rows_t = pl.ds(pl.multiple_of(s * t_len, t_len), t_len)
        gv_bd = jnp.where(stack_diag, jnp.concatenate([z_ref[rows_t, Z_VG:Z_VG + HW]] * GLA_H, axis=0),
                          0.0).astype(bf16)
        og_ref[rows_t, :] = _dot(abig_ref[rows_t, :].astype(bf16), gv_bd)
        for j in range(n_sub):
            rows = pl.ds(pl.multiple_of(s * t_len + j * sub, sub), sub)
            q_bd = jnp.where(head_diag, jnp.concatenate([qt_ref[rows, :]] * GLA_H, axis=0), 0.0).astype(bf16)
            k_bd = jnp.where(head_diag, jnp.concatenate([kh_ref[rows, :]] * GLA_H, axis=0), 0.0).astype(bf16)
            v_st = jnp.concatenate([z_ref[rows, Z_VG + h * HP:Z_VG + (h + 1) * HP] for h in range(GLA_H)],
                                   axis=0).astype(bf16)
            o4 = _dot_nt(q_bd, st.astype(bf16))
            for h in range(GLA_H):
                lanes = slice(h * HP, (h + 1) * HP)
                og_ref[rows, lanes] = og_ref[rows, lanes] + o4[h * sub:(h + 1) * sub, :]
            st = st * gdec_ref[s * n_sub + j] + _dot_tn(v_st, k_bd)
        s_ref[s] = st
        return carry

    lax.fori_loop(0, sb, gla_seq, 0, unroll=sb // SEQ_LOOP_TRIPS)

    def head_norm(o):
        real = lax.broadcasted_iota(jnp.int32, (n_rows, HP), 1) < ML_D
        normed = []
        for h in range(ML_H):
            seg = jnp.where(real, o[:, h * HP:(h + 1) * HP], 0.0)
            ms = jnp.sum(seg * seg, axis=-1, keepdims=True) * (1.0 / ML_D)
            normed.append(seg * lax.rsqrt(ms + EPS))
        return jnp.concatenate(normed, axis=1)

    y_m = (jax.nn.sigmoid(z_ref[:, Z_ZM:Z_ZM + HW])
           * (head_norm(hm_ref[...]) * vec(V_ML_G, HW) + vec(V_ML_SKIP, HW) * cm_ref[...]))
    y_g = head_norm(og_ref[...]) * vec(V_GLA_G, HW) * jax.nn.silu(z_ref[:, Z_GG:Z_GG + HW])
    y = (_dot(yr_ref[...].astype(bf16), wout_ref[0:LRU_W, :])
         + _dot(y_m.astype(bf16), wout_ref[LRU_W:LRU_W + HW, :])
         + _dot(y_g.astype(bf16), wout_ref[LRU_W + HW:, :]))
    y_ref[...] = x_ref[...] + y.reshape(sb, t_len, D_MODEL)

    @pl.when(chunk == pl.num_programs(1) - 1)
    def _store_state():
        tro_ref[...] = extr[:, 0:SUBLANES, :]
        tmo_ref[...] = extm[:, 0:SUBLANES, :]

        def store(s, carry):
            for h in range(ML_H):
                c_ref[s, h] = cst_ref[s, 0:ML_D, h * HP:h * HP + ML_D]
                n_col = cst_ref[s, 0:ML_D, h * HP + ONE_LANE:h * HP + ONE_LANE + 1]
                n_ref[s, h:h + 1, :] = jnp.sum(jnp.where(eye_d, n_col, 0.0), axis=0, keepdims=True)
            return carry

        lax.fori_loop(0, sb, store, 0)


def _mix(x3d, states, donors, lw, consts, *, layer, sb, t_len):
    n_seq, length, _ = x3d.shape
    n_rows = sb * t_len
    n_sub = t_len // min(GLA_SUB, t_len)
    grid = (n_seq // sb, length // t_len)

    def seq_spec(arr):
        blk = (None, sb) + arr.shape[2:]
        zeros = (0,) * (arr.ndim - 2)
        return pl.BlockSpec(blk, lambda i, c: (layer, i) + zeros, pipeline_mode=pl.Buffered(1))

    weights = [lw[k] for k in _MIX_WEIGHT_KEYS]
    assert len(states) == N_STATES and len(consts) == N_MIX_CONSTS
    x_spec = pl.BlockSpec((sb, t_len, D_MODEL), lambda i, c: (i, c, 0))
    row_scratch = lambda width, dtype=f32: pltpu.VMEM((n_rows, width), dtype)
    opnd = bf16 if t_len % BF16_SUBLANES == 0 else f32
    outs = pl.pallas_call(
        functools.partial(_mix_body, sb=sb, t_len=t_len, n_alias=len(donors)),
        out_shape=[jax.ShapeDtypeStruct(x3d.shape, f32)] + [jax.ShapeDtypeStruct(a.shape, f32) for a in states],
        grid=grid,
        in_specs=([x_spec] + [seq_spec(a) for a in states] + [_layer_spec(w.shape, layer) for w in weights]
                  + [_const_spec(c.shape) for c in consts] + [pl.BlockSpec(memory_space=pl.ANY) for _ in donors]),
        out_specs=[x_spec] + [seq_spec(a) for a in states],
        input_output_aliases={N_MIX_INPUTS + k: 1 + k for k in range(len(donors))},
        scratch_shapes=[
            pltpu.VMEM((sb, t_len + SUBLANES, LRU_W), f32),
            pltpu.VMEM((sb, t_len + SUBLANES, HW), f32),
            row_scratch(Z_W),
            row_scratch(HW, opnd),
            row_scratch(HW, opnd),
            row_scratch(HW, opnd),
            row_scratch(HW, opnd),
            row_scratch(ML_H * t_len),
            row_scratch(HW),
            row_scratch(HW),
            row_scratch(LANES),
            pltpu.VMEM((sb, 1, HW), f32),
            row_scratch(HW),
            row_scratch(GLA_KP),
            row_scratch(GLA_KP),
            pltpu.VMEM((sb * n_sub, 1, GLA_KP), f32),
            row_scratch(ML_H * t_len),
            row_scratch(HW),
            row_scratch(LRU_W),
            row_scratch(HW),
            pltpu.VMEM((sb, HP, HW), f32),
        ],
        compiler_params=pltpu.CompilerParams(
            dimension_semantics=("arbitrary", "arbitrary"), vmem_limit_bytes=VMEM_LIMIT_BYTES),
        name="mixer",
    )(x3d, *states, *weights, *consts, *donors)
    return outs[0], list(outs[1:])


def _pad_last(w, width):
    return jnp.pad(w, [(0, 0)] * (w.ndim - 1) + [(0, width - w.shape[-1])])


def _pad_heads(w, d, dp):
    lead = w.shape[:-1]
    w = _pad_last(w.reshape(lead + (w.shape[-1] // d, d)), dp)
    return w.reshape(lead + (-1,))


def _pad_head_rows(w, d, dp):
    return jnp.swapaxes(_pad_heads(jnp.swapaxes(w, -1, -2), d, dp), -1, -2)


def _block_diag(blocks, dp=None):
    depth, n, d, _ = blocks.shape
    dp = d if dp is None else dp
    blocks = jnp.pad(blocks, ((0, 0), (0, 0), (0, dp - d), (0, dp - d)))
    eye = jnp.eye(n, dtype=blocks.dtype)
    return jnp.einsum("lhij,hg->lhigj", blocks, eye).reshape(depth, n * dp, n * dp)


def _vec_rows(v):
    v = v.astype(f32)
    v = v[:, None, :] if v.ndim == 2 else v
    return _pad_last(v, D_MODEL)


def _tail(conv_state, d=None, dp=None):
    conv_state = conv_state.astype(f32)
    if d is not None:
        conv_state = _pad_heads(conv_state, d, dp)
    return jnp.pad(conv_state, ((0, 0), (0, 0), (SUBLANES - (CONV_W - 1), 0), (0, 0)))


def _untail(tail, d=None, dp=None):
    tail = tail[:, :, SUBLANES - (CONV_W - 1):, :]
    if d is not None:
        tail = tail.reshape(tail.shape[:3] + (-1, dp))[..., :d].reshape(tail.shape[:3] + (-1,))
    return tail


def _gla_state_in(s):
    s = jnp.transpose(s.astype(f32), (0, 1, 4, 2, 3))
    s = jnp.pad(s, ((0, 0), (0, 0), (0, HP - GLA_DV), (0, 0), (0, KP - GLA_DK)))
    return s.reshape(s.shape[:2] + (HP, GLA_KP))


def _gla_state_out(st):
    st = st.reshape(st.shape[:2] + (HP, GLA_H, KP))[:, :, :GLA_DV, :, :GLA_DK]
    return jnp.transpose(st, (0, 1, 3, 4, 2))


def _mix_consts(t_len):
    head = jnp.arange(LANES)[:, None]
    lane = jnp.arange(HW)
    e = (head == (lane // HP)[None, :]).astype(bf16)
    e_rt = (head == (jnp.arange(ML_H * t_len) // t_len)[None, :]).astype(bf16)
    return jnp.concatenate([e, e], axis=0), jnp.concatenate([e_rt, e_rt], axis=0)


def kernel(x_prompt, x_sample, state_lru_h, state_lru_conv, state_mlstm_C, state_mlstm_n, state_mlstm_m, state_mlstm_conv, state_gla_S, meta_tokens, ffn1_norm_g, ffn1_w1, ffn1_w3, ffn1_w2, mix_norm_g, w_in, lru_conv_w, lru_conv_b, lru_wa, lru_ba, lru_wx, lru_bx, lru_lambda, ml_conv_w, ml_conv_b, ml_wq, ml_wk, ml_wv, ml_w_if, ml_b_if, ml_norm_g, ml_skip, gla_w_up, gla_b_up, gla_norm_g, w_out, ffn2_norm_g, ffn2_w1, ffn2_w3, ffn2_w2, final_norm_g):
    n_prompt = x_prompt.shape[0]
    n_sample, t_sample, _ = x_sample.shape

    sizes = (LRU_W, LRU_W, ML_W, ML_W, GLA_KW, GLA_KW, GLA_VW, GLA_VW, GLA_RANK)
    pads = (None, None, (ML_D, HP), (ML_D, HP), (GLA_DK, KP), (GLA_DK, KP), (GLA_DV, HP), (GLA_DV, HP), None)
    parts, off = [], 0
    for size, pad in zip(sizes, pads):
        part = w_in[..., off:off + size]
        parts.append(part if pad is None else _pad_heads(part, *pad))
        off += size
    parts[-1] = _pad_last(parts[-1], LANES)
    wif = jnp.concatenate([_pad_head_rows(ml_w_if[:, i * ML_W:(i + 1) * ML_W], ML_D, HP) for i in range(3)],
                          axis=1)
    wif = jnp.concatenate([_pad_last(wif[..., :ML_H], LANES), _pad_last(wif[..., ML_H:], LANES)], axis=-1)
    bif = jnp.concatenate([_pad_last(ml_b_if[:, :ML_H], LANES), _pad_last(ml_b_if[:, ML_H:], LANES)], axis=-1)
    wo = jnp.concatenate([w_out[:, :LRU_W], _pad_head_rows(w_out[:, LRU_W:LRU_W + ML_W], ML_D, HP),
                          _pad_head_rows(w_out[:, LRU_W + ML_W:], GLA_DV, HP)], axis=1)
    ones_row = jnp.broadcast_to(((jnp.arange(HW) % HP) == ONE_LANE).astype(f32), (DEPTH, HW))
    vec_rows = [mix_norm_g, lru_conv_w, lru_conv_b, lru_ba, lru_bx, lru_lambda,
                _pad_heads(ml_conv_w, ML_D, HP), _pad_heads(ml_conv_b, ML_D, HP), bif,
                _pad_heads(ml_norm_g, ML_D, HP), _pad_heads(ml_skip, ML_D, HP),
                _pad_heads(gla_b_up, GLA_DK, KP), _pad_heads(gla_norm_g, GLA_DV, HP), ones_row]
    vecs = jnp.concatenate([_vec_rows(v) for v in vec_rows], axis=1)
    vecs = jnp.pad(vecs, ((0, 0), (0, VEC_ROWS - vecs.shape[1]), (0, 0)))
    lw = dict(
        vecs=vecs,
        w_in=jnp.concatenate(parts, axis=-1).astype(bf16),
        lru_wa=_block_diag(lru_wa).astype(bf16), lru_wx=_block_diag(lru_wx).astype(bf16),
        ml_wq=_block_diag(ml_wq, HP).astype(bf16), ml_wk=_block_diag(ml_wk, HP).astype(bf16),
        ml_wv=_block_diag(ml_wv, HP).astype(bf16), ml_wif=wif.astype(bf16),
        gla_wup=jnp.pad(_pad_heads(gla_w_up, GLA_DK, KP), ((0, 0), (0, LANES - GLA_RANK), (0, 0))).astype(bf16),
        w_out=wo.astype(bf16))
    ffn1 = (ffn1_norm_g.astype(f32)[:, None, :], ffn1_w1.astype(bf16), ffn1_w3.astype(bf16), ffn1_w2.astype(bf16))
    ffn2 = (ffn2_norm_g.astype(f32)[:, None, :], ffn2_w1.astype(bf16), ffn2_w3.astype(bf16), ffn2_w2.astype(bf16))
    final_g = final_norm_g.astype(f32)[None, :]

    def run_trunk(x3d, states, *, sb, t_len, tm, need_output):
        n_seq, length, _ = x3d.shape
        consts = _mix_consts(t_len)
        new_states = ()
        for l in range(DEPTH):
            x2d = _ffn(x3d.reshape(n_seq * length, D_MODEL), *ffn1, final_g, layer=l, tm=tm, post_norm=False)
            x3d, new_states = _mix(x2d.reshape(n_seq, length, D_MODEL), states, new_states, lw, consts,
                                   layer=l, sb=sb, t_len=t_len)
            last = l == DEPTH - 1
            if need_output or not last:
                x2d = _ffn(x3d.reshape(n_seq * length, D_MODEL), *ffn2, final_g, layer=l, tm=tm, post_norm=last)
                x3d = x2d.reshape(n_seq, length, D_MODEL)
        return x3d, new_states

    zeros = lambda *shape: jnp.zeros((DEPTH, n_prompt) + shape, f32)
    zero_states = [zeros(LRU_W), zeros(SUBLANES, LRU_W), zeros(SUBLANES, HW), zeros(ML_H, ML_D, ML_D),
                   zeros(ML_H, ML_D), zeros(1, LANES), zeros(HP, GLA_KP)]
    x_meta = jnp.broadcast_to(meta_tokens.astype(f32)[None], (n_prompt, N_META, D_MODEL))
    _, meta_states = run_trunk(x_meta, zero_states, sb=n_prompt, t_len=N_META, tm=n_prompt * N_META,
                               need_output=False)

    y_prompt, p_states = run_trunk(x_prompt, meta_states, sb=n_prompt, t_len=CHUNK, tm=512, need_output=True)

    s_states = [state_lru_h.astype(f32), _tail(state_lru_conv), _tail(state_mlstm_conv, ML_D, HP),
                state_mlstm_C.astype(f32), state_mlstm_n.astype(f32),
                _pad_last(state_mlstm_m.astype(f32), LANES)[:, :, None, :], _gla_state_in(state_gla_S)]
    y_sample, s_states = run_trunk(x_sample, s_states, sb=16, t_len=t_sample, tm=512, need_output=True)

    def reference_layout(st):
        lru_h, lru_tail, ml_tail, ml_c, ml_n, ml_m, gla_s = st
        return (lru_h, _untail(lru_tail), ml_c, ml_n, ml_m[:, :, 0, :ML_H], _untail(ml_tail, ML_D, HP),
                _gla_state_out(gla_s))

    return (y_prompt, y_sample) + reference_layout(p_states) + reference_layout(s_states)
```

```python
import functools

import jax
import jax.numpy as jnp
from jax import lax
from jax.experimental import pallas as pl
from jax.experimental.pallas import tpu as pltpu

f32 = jnp.float32
bf16 = jnp.bfloat16

D_MODEL = 1024
DEPTH = 2
N_META = 16
CONV_W = 4
CHUNK = 64
LRU_W = 256
LRU_BLOCKS = 4
LRU_C = 8.0
ML_H = 4
ML_D = 96
ML_W = ML_H * ML_D
GLA_H = 4
GLA_DK = 48
GLA_DV = 96
GLA_KW = GLA_H * GLA_DK
GLA_VW = GLA_H * GLA_DV
GLA_RANK = 16
GLA_TAU = 16.0
D_FF = 2816
EPS = 1e-6

SUBLANES = 8
LANES = 128
BF16_SUBLANES = 16
VMEM_LIMIT_BYTES = 60 * 1024 * 1024

FF_TILE = 256
N_FF_TILES = D_FF // FF_TILE
GLA_SUB = 16
SEQ_LOOP_TRIPS = 2

HP = LANES
HW = ML_H * HP
ONE_LANE = ML_D
GLA_KP = 2 * LANES

Z_UR = 0
Z_GR = Z_UR + LRU_W
Z_UM = Z_GR + LRU_W
Z_ZM = Z_UM + HW
Z_QG = Z_ZM + HW
Z_KG = Z_QG + GLA_KP
Z_VG = Z_KG + GLA_KP
Z_GG = Z_VG + HW
Z_AL = Z_GG + HW
Z_W = Z_AL + LANES


def _dot(a, b):
    return jnp.dot(a, b, preferred_element_type=f32)


def _dot_nt(a, b):
    return lax.dot_general(a, b, (((1,), (1,)), ((), ())), preferred_element_type=f32)


def _dot_tn(a, b):
    return lax.dot_general(a, b, (((0,), (0,)), ((), ())), preferred_element_type=f32)


def _rms(x, g):
    return x * lax.rsqrt(jnp.mean(x * x, axis=-1, keepdims=True) + EPS) * g


def _ffn_body(x_ref, g_ref, w1_ref, w3_ref, w2_ref, gf_ref, o_ref, gated_ref, *, post_norm):
    x = x_ref[...]
    h = _rms(x, g_ref[...]).astype(bf16)
    for f in range(N_FF_TILES):
        cols = slice(f * FF_TILE, (f + 1) * FF_TILE)
        a = _dot(h, w1_ref[:, cols])
        b = _dot(h, w3_ref[:, cols])
        gated_ref[:, cols] = (a * jax.nn.sigmoid(a) * b).astype(bf16)
    y = x + 0.5 * _dot(gated_ref[...], w2_ref[...])
    if post_norm:
        y = _rms(y, gf_ref[...])
    o_ref[...] = y


def _const_spec(shape):
    zeros = (0,) * len(shape)
    return pl.BlockSpec(shape, lambda *_: zeros, pipeline_mode=pl.Buffered(1))


def _layer_spec(shape, layer):
    zeros = (0,) * (len(shape) - 1)
    return pl.BlockSpec((None,) + tuple(shape[1:]), lambda *_: (layer,) + zeros, pipeline_mode=pl.Buffered(1))


def _ffn(x2d, g, w1, w3, w2, gf, *, layer, tm, post_norm):
    rows = x2d.shape[0]
    return pl.pallas_call(
        functools.partial(_ffn_body, post_norm=post_norm),
        out_shape=jax.ShapeDtypeStruct(x2d.shape, f32),
        grid=(rows // tm,),
        in_specs=[
            pl.BlockSpec((tm, D_MODEL), lambda i: (i, 0)),
            _layer_spec(g.shape, layer),
            _layer_spec(w1.shape, layer),
            _layer_spec(w3.shape, layer),
            _layer_spec(w2.shape, layer),
            _const_spec(gf.shape),
        ],
        out_specs=pl.BlockSpec((tm, D_MODEL), lambda i: (i, 0)),
        scratch_shapes=[pltpu.VMEM((tm, D_FF), bf16)],
        compiler_params=pltpu.CompilerParams(
            dimension_semantics=("arbitrary",), vmem_limit_bytes=VMEM_LIMIT_BYTES),
        name="ffn",
    )(x2d, g, w1, w3, w2, gf)


def _row_time(shape, period):
    return lax.broadcasted_iota(jnp.int32, shape, 0) % period


def _seg_cumsum(x, period):
    t = _row_time(x.shape, period)
    s = 1
    while s < period:
        x = x + jnp.where(t >= s, pltpu.roll(x, s, 0), 0.0)
        s *= 2
    return x


def _rows_from_seq(v, t_len):
    sb, width = v.shape
    return jnp.broadcast_to(v[:, None, :], (sb, t_len, width)).reshape(sb * t_len, width)


def _last_rows(x, groups, period):
    return x.reshape(groups, period, x.shape[-1])[:, period - 1, :]


def _key_head(idx):
    head = jnp.zeros_like(idx)
    for h in range(1, GLA_H + 1):
        head = head + (idx >= h * GLA_DK).astype(jnp.int32)
    return head


def _expand(x, e2):
    hi = x.astype(bf16)
    lo = (x - hi.astype(f32)).astype(bf16)
    return _dot(jnp.concatenate([hi, lo], axis=1), e2)


def _causal_conv(ext_ref, u, w, b, sb, t_len):
    width = u.shape[-1]
    ext_ref[:, SUBLANES:SUBLANES + t_len, :] = u.reshape(sb, t_len, width)
    base = SUBLANES - (CONV_W - 1)
    out = b + ext_ref[:, base:base + t_len, :] * w[0:1, :]
    for j in range(1, CONV_W):
        out = out + ext_ref[:, base + j:base + j + t_len, :] * w[j:j + 1, :]
    ext_ref[:, 0:SUBLANES, :] = ext_ref[:, t_len:t_len + SUBLANES, :]
    return out.reshape(sb * t_len, width)


V_MIX_G = 0
V_LRU_CW = 1
V_LRU_CB = 5
V_LRU_BA = 6
V_LRU_BX = 7
V_LRU_LAM = 8
V_ML_CW = 9
V_ML_CB = 13
V_ML_BIF = 14
V_ML_G = 15
V_ML_SKIP = 16
V_GLA_BUP = 17
V_GLA_G = 18
V_ONES = 19
VEC_ROWS = 24
N_STATES = 7
_MIX_WEIGHT_KEYS = ("vecs", "w_in", "lru_wa", "lru_wx", "ml_wq", "ml_wk", "ml_wv", "ml_wif", "gla_wup", "w_out")
N_MIX_CONSTS = 2
N_MIX_INPUTS = 1 + N_STATES + len(_MIX_WEIGHT_KEYS) + N_MIX_CONSTS


def _mix_body(*refs, sb, t_len, n_alias):
    (x_ref, h0_ref, tr0_ref, tm0_ref, c0_ref, n0_ref, m0_ref, s0_ref,
     vec_ref, win_ref, wa_ref, wx_ref, wq_ref, wk_ref, wv_ref, wif_ref, wup_ref, wout_ref,
     e2_ref, ert2_ref) = refs[:N_MIX_INPUTS]
    (y_ref, h_ref, tro_ref, tmo_ref, c_ref, n_ref, m_ref, s_ref,
     extr, extm, z_ref, mq_ref, mk_ref, kw_ref, va_ref, rtx_ref, wix_ref, eix_ref, cc_ref,
     decx_ref, hm_ref, qt_ref, kh_ref, gdec_ref, abig_ref, og_ref, yr_ref, cm_ref,
     cst_ref, gst_ref) = refs[N_MIX_INPUTS + n_alias:]

    def vec(row, width, n=1):
        return vec_ref[row:row + n, 0:width]

    chunk = pl.program_id(1)
    n_rows = sb * t_len
    eye_d = (lax.broadcasted_iota(jnp.int32, (ML_D, ML_D), 0)
             == lax.broadcasted_iota(jnp.int32, (ML_D, ML_D), 1))

    @pl.when(chunk == 0)
    def _load_state():
        h_ref[...] = h0_ref[...]
        extr[:, 0:SUBLANES, :] = tr0_ref[...]
        extm[:, 0:SUBLANES, :] = tm0_ref[...]
        m_ref[...] = m0_ref[...]

        def load(s, carry):
            cst_ref[s] = jnp.zeros((HP, HW), f32)
            gst_ref[s] = jnp.zeros((GLA_KP, HP), f32)
            for h in range(GLA_H):
                gst_ref[s, h * GLA_DK:(h + 1) * GLA_DK, 0:GLA_DV] = s0_ref[s, h]
            for h in range(ML_H):
                cst_ref[s, 0:ML_D, h * HP:h * HP + ML_D] = c0_ref[s, h]
                n_col = jnp.sum(jnp.where(eye_d, n0_ref[s, h:h + 1, :], 0.0), axis=1, keepdims=True)
                cst_ref[s, 0:ML_D, h * HP + ONE_LANE:h * HP + ONE_LANE + 1] = n_col
            return carry

        lax.fori_loop(0, sb, load, 0)

    x = x_ref[...].reshape(n_rows, D_MODEL)
    hn = _rms(x, vec(V_MIX_G, D_MODEL)).astype(bf16)
    z_ref[...] = _dot(hn, win_ref[...])

    xr = _causal_conv(extr, z_ref[:, Z_UR:Z_UR + LRU_W], vec(V_LRU_CW, LRU_W, CONV_W), vec(V_LRU_CB, LRU_W),
                      sb, t_len)
    xr_b = xr.astype(bf16)
    r = jax.nn.sigmoid(_dot(xr_b, wa_ref[...]) + vec(V_LRU_BA, LRU_W))
    ig = jax.nn.sigmoid(_dot(xr_b, wx_ref[...]) + vec(V_LRU_BX, LRU_W))
    log_a = -LRU_C * r * jax.nn.softplus(-vec(V_LRU_LAM, LRU_W))
    a = jnp.exp(log_a)
    one_m_a2 = 1.0 - a * a
    root = jnp.where(one_m_a2 > 0.0, one_m_a2 * lax.rsqrt(one_m_a2), 0.0)
    bt = root * (ig * xr)
    tl = _row_time((n_rows, LRU_W), t_len)
    s = 1
    while s < t_len:
        keep = tl >= s
        a_prev = jnp.where(keep, pltpu.roll(a, s, 0), 1.0)
        b_prev = jnp.where(keep, pltpu.roll(bt, s, 0), 0.0)
        bt = a * b_prev + bt
        a = a * a_prev
        s *= 2
    hs = bt + a * _rows_from_seq(h_ref[...], t_len)
    h_ref[...] = _last_rows(hs, sb, t_len)
    yr_ref[...] = jax.nn.gelu(z_ref[:, Z_GR:Z_GR + LRU_W]) * hs

    u_m = z_ref[:, Z_UM:Z_UM + HW]
    cm = jax.nn.silu(_causal_conv(extm, u_m, vec(V_ML_CW, HW, CONV_W), vec(V_ML_CB, HW), sb, t_len))
    cm_ref[...] = cm
    cm_b = cm.astype(bf16)
    mq = _dot(cm_b, wq_ref[...])
    mk = _dot(cm_b, wk_ref[...])
    va = _dot(u_m.astype(bf16), wv_ref[...]) + vec(V_ONES, HW)
    gates = _dot(jnp.concatenate([mq.astype(bf16), mk.astype(bf16), va.astype(bf16)], axis=1),
                 wif_ref[...]) + vec(V_ML_BIF, 2 * LANES)
    li = gates[:, 0:LANES]
    lf = jax.nn.log_sigmoid(gates[:, LANES:2 * LANES])
    tg = _row_time((n_rows, LANES), t_len)
    bcum, m_loc = lf, li
    s = 1
    while s < t_len:
        keep = tg >= s
        b_prev = jnp.where(keep, pltpu.roll(bcum, s, 0), 0.0)
        m_prev_seg = jnp.where(keep, pltpu.roll(m_loc, s, 0), -jnp.inf)
        m_loc = jnp.maximum(m_prev_seg + bcum, m_loc)
        bcum = bcum + b_prev
        s *= 2
    m_old = m_ref[:, 0, :]
    m_old_rows = _rows_from_seq(m_old, t_len)
    m_t = jnp.maximum(bcum + m_old_rows, m_loc)
    m_new = _last_rows(m_t, sb, t_len)
    b_last = _last_rows(bcum, sb, t_len)
    m_ref[...] = m_new[:, None, :]
    cc = li - bcum
    cc_ref[...] = cc
    e2 = e2_ref[...]
    rtx_ref[...] = _expand(bcum - m_t, ert2_ref[...])
    wix_ref[...] = jnp.exp(_expand(bcum + m_old_rows - m_t, e2))
    eix_ref[...] = jnp.exp(_expand(-m_t, e2))
    w_k = jnp.exp(_expand(_rows_from_seq(b_last, t_len) + cc - _rows_from_seq(m_new, t_len), e2))
    decx_ref[...] = jnp.exp(_expand(b_last + m_old - m_new, e2))[:, None, :]
    mq_ref[...] = (mq * (ML_D ** -0.5)).astype(mq_ref.dtype)
    mk_ref[...] = mk.astype(mk_ref.dtype)
    kw_ref[...] = (mk * w_k).astype(kw_ref.dtype)
    va_ref[...] = va.astype(va_ref.dtype)

    t4 = ML_H * t_len
    causal_cat = (lax.broadcasted_iota(jnp.int32, (t_len, t4), 0)
                  >= lax.broadcasted_iota(jnp.int32, (t_len, t4), 1) % t_len)
    stack_diag = (lax.broadcasted_iota(jnp.int32, (t4, HW), 0) // t_len
                  == lax.broadcasted_iota(jnp.int32, (t4, HW), 1) // HP)
    state_diag = (lax.broadcasted_iota(jnp.int32, (HW, HW), 0) // HP
                  == lax.broadcasted_iota(jnp.int32, (HW, HW), 1) // HP)
    lane_head = lax.broadcasted_iota(jnp.int32, (t_len, HW), 1) // HP

    def mlstm_seq(s, carry):
        rows = pl.ds(pl.multiple_of(s * t_len, t_len), t_len)
        q_b = mq_ref[rows, :].astype(bf16)
        k_bd = jnp.where(stack_diag, jnp.concatenate([mk_ref[rows, :]] * ML_H, axis=0), 0.0).astype(bf16)
        v_bd = jnp.where(stack_diag, jnp.concatenate([va_ref[rows, :]] * ML_H, axis=0), 0.0).astype(bf16)
        c_t = cc_ref[rows, :].T
        c_row = jnp.concatenate([c_t[h:h + 1, :] for h in range(ML_H)], axis=1)
        dmat = jnp.where(causal_cat, rtx_ref[rows, :] + c_row, -jnp.inf)
        sc = _dot_nt(q_b, k_bd) * jnp.exp(dmat)
        c_old = cst_ref[s]
        c_bd = jnp.where(state_diag, jnp.concatenate([c_old.astype(bf16)] * ML_H, axis=0), 0.0)
        num = _dot(sc.astype(bf16), v_bd) + _dot(q_b, c_bd) * wix_ref[rows, :]
        den = jnp.zeros((t_len, HW), f32)
        for h in range(ML_H):
            den = jnp.where(lane_head == h, num[:, h * HP + ONE_LANE:h * HP + ONE_LANE + 1], den)
        hm_ref[rows, :] = num / jnp.maximum(jnp.abs(den), eix_ref[rows, :])
        kw_st = jnp.concatenate([kw_ref[rows, h * HP:(h + 1) * HP] for h in range(ML_H)],
                                axis=0).astype(bf16)
        cst_ref[s] = c_old * decx_ref[s] + _dot_tn(kw_st, v_bd)
        return carry

    lax.fori_loop(0, sb, mlstm_seq, 0, unroll=sb // SEQ_LOOP_TRIPS)

    sub = min(GLA_SUB, t_len)
    n_sub = t_len // sub
    al_b = z_ref[:, Z_AL:Z_AL + LANES].astype(bf16)
    lg = jax.nn.log_sigmoid(_dot(al_b, wup_ref[...]) + vec(V_GLA_BUP, GLA_KP)) * (1.0 / GLA_TAU)
    bcl = _seg_cumsum(lg, sub)
    gq = z_ref[:, Z_QG:Z_QG + GLA_KP] * (GLA_DK ** -0.5)
    gk = z_ref[:, Z_KG:Z_KG + GLA_KP]
    last = _last_rows(bcl, n_rows // sub, sub)
    dec = jnp.exp(last).reshape(sb, n_sub, GLA_KP)
    gdec_ref[...] = jnp.concatenate([dec, jnp.ones((sb, SUBLANES - n_sub, GLA_KP), f32)],
                                    axis=1).reshape(sb * SUBLANES, GLA_KP)
    qt_ref[...] = gq * jnp.exp(bcl)
    kh_ref[...] = gk * jnp.exp(_rows_from_seq(last, sub) - bcl)
    n_grp = n_rows // sub

    def grp_row(arr, i):
        picked = arr.reshape(n_grp, sub, arr.shape[-1])[:, i, :]
        return jnp.broadcast_to(picked[:, None, :], (n_grp, sub, arr.shape[-1])).reshape(arr.shape)

    ts = _row_time((n_rows, GLA_KP), sub)
    key_head_r = _key_head(lax.broadcasted_iota(jnp.int32, (GLA_KP, LANES), 0))
    out_lane =lax.broadcasted_iota(jnp.int32, (GLA_KP, LANES), 1)
    out_src = jnp.where(key_head_r == out_lane // sub, out_lane % sub, -1)
    a_cat = jnp.zeros((n_rows, LANES), f32)
    for i in range(sub):
        e = jnp.where(ts >= i, gq * grp_row(gk, i) * jnp.exp(bcl - grp_row(bcl, i)), 0.0)
        a_cat = a_cat + _dot(e.astype(bf16), jnp.where(out_src == i, 1.0, 0.0).astype(bf16))
    ex_r = lax.broadcasted_iota(jnp.int32, (LANES, t4), 0)
    ex_l = lax.broadcasted_iota(jnp.int32, (LANES, t4), 1)
    ex = jnp.where(ex_r // sub == ex_l // t_len, jnp.where(ex_r % sub == ex_l % sub, 1.0, 0.0), 0.0).astype(bf16)
    row_grp = (lax.broadcasted_iota(jnp.int32, (n_rows, t4), 0) % t_len) // sub
    lane_grp = (lax.broadcasted_iota(jnp.int32, (n_rows, t4), 1) % t_len) // sub
    abig_ref[...] = jnp.where(row_grp == lane_grp, _dot(a_cat.astype(bf16), ex), 0.0)

    stack_head = lax.broadcasted_iota(jnp.int32, (GLA_H * sub, GLA_KP), 0) // sub
    key_head = _key_head(lax.broadcasted_iota(jnp.int32, (GLA_H * sub, GLA_KP), 1))
    head_diag = stack_head == key_head

    def gla_seq(s, carry):
        st = gst_ref[s]
        dec_t = gdec_ref[pl.ds(pl.multiple_of(s * SUBLANES, SUBLANES), SUBLANES), :].T
        rows_t = pl.ds(pl.multiple_of(s * t_len, t_len), t_len)
        gv_bd = jnp.where(stack_diag, jnp.concatenate([z_ref[rows_t, Z_VG:Z_VG + HW]] * GLA_H, axis=0),
                          0.0).astype(bf16)
        og_ref[rows_t, :] = _dot(abig_ref[rows_t, :].astype(bf16), gv_bd)
        for j in range(n_sub):
            rows = pl.ds(pl.multiple_of(s * t_len + j * sub, sub), sub)
            q_bd = jnp.where(head_diag, jnp.concatenate([qt_ref[rows, :]] * GLA_H, axis=0), 0.0).astype(bf16)
            k_bd = jnp.where(head_diag, jnp.concatenate([kh_ref[rows, :]] * GLA_H, axis=0), 0.0).astype(bf16)
            v_st = jnp.concatenate([z_ref[rows, Z_VG + h * HP:Z_VG + (h + 1) * HP] for h in range(GLA_H)],
                                   axis=0).astype(bf16)
            o4 = _dot(q_bd, st.astype(bf16))
            for h in range(GLA_H):
                lanes = slice(h * HP, (h + 1) * HP)
                og_ref[rows, lanes] = og_ref[rows, lanes] + o4[h * sub:(h + 1) * sub, :]
            st = st * dec_t[:, j:j + 1] + _dot_tn(k_bd, v_st)
        gst_ref[s] = st
        return carry

    lax.fori_loop(0, sb, gla_seq, 0, unroll=sb // SEQ_LOOP_TRIPS)

    def head_norm(o):
        real = lax.broadcasted_iota(jnp.int32, (n_rows, HP), 1) < ML_D
        normed = []
        for h in range(ML_H):
            seg = jnp.where(real, o[:, h * HP:(h + 1) * HP], 0.0)
            ms = jnp.sum(seg * seg, axis=-1, keepdims=True) * (1.0 / ML_D)
            normed.append(seg * lax.rsqrt(ms + EPS))
        return jnp.concatenate(normed, axis=1)

    y_m = (jax.nn.sigmoid(z_ref[:, Z_ZM:Z_ZM + HW])
           * (head_norm(hm_ref[...]) * vec(V_ML_G, HW) + vec(V_ML_SKIP, HW) * cm_ref[...]))
    y_g = head_norm(og_ref[...]) * vec(V_GLA_G, HW) * jax.nn.silu(z_ref[:, Z_GG:Z_GG + HW])
    y = (_dot(yr_ref[...].astype(bf16), wout_ref[0:LRU_W, :])
         + _dot(y_m.astype(bf16), wout_ref[LRU_W:LRU_W + HW, :])
         + _dot(y_g.astype(bf16), wout_ref[LRU_W + HW:, :]))
    y_ref[...] = x_ref[...] + y.reshape(sb, t_len, D_MODEL)

    @pl.when(chunk == pl.num_programs(1) - 1)
    def _store_state():
        tro_ref[...] = extr[:, 0:SUBLANES, :]
        tmo_ref[...] = extm[:, 0:SUBLANES, :]

        def store(s, carry):
            for h in range(GLA_H):
                s_ref[s, h] = gst_ref[s, h * GLA_DK:(h + 1) * GLA_DK, 0:GLA_DV]
            for h in range(ML_H):
                c_ref[s, h] = cst_ref[s, 0:ML_D, h * HP:h * HP + ML_D]
                n_col = cst_ref[s, 0:ML_D, h * HP + ONE_LANE:h * HP + ONE_LANE + 1]
                n_ref[s, h:h + 1, :] = jnp.sum(jnp.where(eye_d, n_col, 0.0), axis=0, keepdims=True)
            return carry

        lax.fori_loop(0, sb, store, 0)


def _mix(x3d, states, donors, lw, consts, *, layer, sb, t_len):
    n_seq, length, _ = x3d.shape
    n_rows = sb * t_len
    n_sub = t_len // min(GLA_SUB, t_len)
    grid = (n_seq // sb, length // t_len)

    def seq_spec(arr):
        blk = (None, sb) + arr.shape[2:]
        zeros = (0,) * (arr.ndim - 2)
        return pl.BlockSpec(blk, lambda i, c: (layer, i) + zeros, pipeline_mode=pl.Buffered(1))

    weights = [lw[k] for k in _MIX_WEIGHT_KEYS]
    assert len(states) == N_STATES and len(consts) == N_MIX_CONSTS
    x_spec = pl.BlockSpec((sb, t_len, D_MODEL), lambda i, c: (i, c, 0))
    row_scratch = lambda width, dtype=f32: pltpu.VMEM((n_rows, width), dtype)
    opnd = bf16 if t_len % BF16_SUBLANES == 0 else f32
    outs = pl.pallas_call(
        functools.partial(_mix_body, sb=sb, t_len=t_len, n_alias=len(donors)),
        out_shape=[jax.ShapeDtypeStruct(x3d.shape, f32)] + [jax.ShapeDtypeStruct(a.shape, f32) for a in states],
        grid=grid,
        in_specs=([x_spec] + [seq_spec(a) for a in states] + [_layer_spec(w.shape, layer) for w in weights]
                  + [_const_spec(c.shape) for c in consts] + [pl.BlockSpec(memory_space=pl.ANY) for _ in donors]),
        out_specs=[x_spec] + [seq_spec(a) for a in states],
        input_output_aliases={N_MIX_INPUTS + k: 1 + k for k in range(len(donors))},
        scratch_shapes=[
            pltpu.VMEM((sb, t_len + SUBLANES, LRU_W), f32),
            pltpu.VMEM((sb, t_len + SUBLANES, HW), f32),
            row_scratch(Z_W),
            row_scratch(HW, opnd),
            row_scratch(HW, opnd),
            row_scratch(HW, opnd),
            row_scratch(HW, opnd),
            row_scratch(ML_H * t_len),
            row_scratch(HW),
            row_scratch(HW),
            row_scratch(LANES),
            pltpu.VMEM((sb, 1, HW), f32),
            row_scratch(HW),
            row_scratch(GLA_KP),
            row_scratch(GLA_KP),
            pltpu.VMEM((sb * SUBLANES, GLA_KP), f32),
            row_scratch(ML_H * t_len),
            row_scratch(HW),
            row_scratch(LRU_W),
            row_scratch(HW),
            pltpu.VMEM((sb, HP, HW), f32),
            pltpu.VMEM((sb, GLA_KP, HP), f32),
        ],
        compiler_params=pltpu.CompilerParams(
            dimension_semantics=("arbitrary", "arbitrary"), vmem_limit_bytes=VMEM_LIMIT_BYTES),
        name="mixer",
    )(x3d, *states, *weights, *consts, *donors)
    return outs[0], list(outs[1:])


def _pad_last(w, width):
    return jnp.pad(w, [(0, 0)] * (w.ndim - 1) + [(0, width - w.shape[-1])])


def _pad_heads(w, d, dp):
    lead = w.shape[:-1]
    w = _pad_last(w.reshape(lead + (w.shape[-1] // d, d)), dp)
    return w.reshape(lead + (-1,))


def _pad_head_rows(w, d, dp):
    return jnp.swapaxes(_pad_heads(jnp.swapaxes(w, -1, -2), d, dp), -1, -2)


def _block_diag(blocks, dp=None):
    depth, n, d, _ = blocks.shape
    dp = d if dp is None else dp
    blocks = jnp.pad(blocks, ((0, 0), (0, 0), (0, dp - d), (0, dp - d)))
    eye = jnp.eye(n, dtype=blocks.dtype)
    return jnp.einsum("lhij,hg->lhigj", blocks, eye).reshape(depth, n * dp, n * dp)


def _vec_rows(v):
    v = v.astype(f32)
    v = v[:, None, :] if v.ndim == 2 else v
    return _pad_last(v, D_MODEL)


def _tail(conv_state, d=None, dp=None):
    conv_state = conv_state.astype(f32)
    if d is not None:
        conv_state = _pad_heads(conv_state, d, dp)
    return jnp.pad(conv_state, ((0, 0), (0, 0), (SUBLANES - (CONV_W - 1), 0), (0, 0)))


def _untail(tail, d=None, dp=None):
    tail = tail[:, :, SUBLANES - (CONV_W - 1):, :]
    if d is not None:
        tail = tail.reshape(tail.shape[:3] + (-1, dp))[..., :d].reshape(tail.shape[:3] + (-1,))
    return tail


def _mix_consts(t_len):
    head = jnp.arange(LANES)[:, None]
    lane = jnp.arange(HW)
    e = (head == (lane // HP)[None, :]).astype(bf16)
    e_rt = (head == (jnp.arange(ML_H * t_len) // t_len)[None, :]).astype(bf16)
    return jnp.concatenate([e, e], axis=0), jnp.concatenate([e_rt, e_rt], axis=0)


def kernel(x_prompt, x_sample, state_lru_h, state_lru_conv, state_mlstm_C, state_mlstm_n, state_mlstm_m, state_mlstm_conv, state_gla_S, meta_tokens, ffn1_norm_g, ffn1_w1, ffn1_w3, ffn1_w2, mix_norm_g, w_in, lru_conv_w, lru_conv_b, lru_wa, lru_ba, lru_wx, lru_bx, lru_lambda, ml_conv_w, ml_conv_b, ml_wq, ml_wk, ml_wv, ml_w_if, ml_b_if, ml_norm_g, ml_skip, gla_w_up, gla_b_up, gla_norm_g, w_out, ffn2_norm_g, ffn2_w1, ffn2_w3, ffn2_w2, final_norm_g):
    n_prompt = x_prompt.shape[0]
    n_sample, t_sample, _ = x_sample.shape

    sizes = (LRU_W, LRU_W, ML_W, ML_W, GLA_KW, GLA_KW, GLA_VW, GLA_VW, GLA_RANK)
    widths = (LRU_W, LRU_W, HW, HW, GLA_KP, GLA_KP, HW, HW, LANES)
    head_pads = (None, None, (ML_D, HP), (ML_D, HP), None, None, (GLA_DV, HP), (GLA_DV, HP), None)
    parts, off = [], 0
    for size, width, pad in zip(sizes, widths, head_pads):
        part = w_in[..., off:off + size]
        parts.append(_pad_last(part, width) if pad is None else _pad_heads(part, *pad))
        off += size
    wif = jnp.concatenate([_pad_head_rows(ml_w_if[:, i * ML_W:(i + 1) * ML_W], ML_D, HP) for i in range(3)],
                          axis=1)
    wif = jnp.concatenate([_pad_last(wif[..., :ML_H], LANES), _pad_last(wif[..., ML_H:], LANES)], axis=-1)
    bif = jnp.concatenate([_pad_last(ml_b_if[:, :ML_H], LANES), _pad_last(ml_b_if[:, ML_H:], LANES)], axis=-1)
    wo = jnp.concatenate([w_out[:, :LRU_W], _pad_head_rows(w_out[:, LRU_W:LRU_W + ML_W], ML_D, HP),
                          _pad_head_rows(w_out[:, LRU_W + ML_W:], GLA_DV, HP)], axis=1)
    ones_row = jnp.broadcast_to(((jnp.arange(HW) % HP) == ONE_LANE).astype(f32), (DEPTH, HW))
    vec_rows = [mix_norm_g, lru_conv_w, lru_conv_b, lru_ba, lru_bx, lru_lambda,
                _pad_heads(ml_conv_w, ML_D, HP), _pad_heads(ml_conv_b, ML_D, HP), bif,
                _pad_heads(ml_norm_g, ML_D, HP), _pad_heads(ml_skip, ML_D, HP),
                gla_b_up, _pad_heads(gla_norm_g, GLA_DV, HP), ones_row]
    vecs = jnp.concatenate([_vec_rows(v) for v in vec_rows], axis=1)
    vecs = jnp.pad(vecs, ((0, 0), (0, VEC_ROWS - vecs.shape[1]), (0, 0)))
    lw = dict(
        vecs=vecs,
        w_in=jnp.concatenate(parts, axis=-1).astype(bf16),
        lru_wa=_block_diag(lru_wa).astype(bf16), lru_wx=_block_diag(lru_wx).astype(bf16),
        ml_wq=_block_diag(ml_wq, HP).astype(bf16), ml_wk=_block_diag(ml_wk, HP).astype(bf16),
        ml_wv=_block_diag(ml_wv, HP).astype(bf16), ml_wif=wif.astype(bf16),
        gla_wup=jnp.pad(gla_w_up, ((0, 0), (0, LANES - GLA_RANK), (0, GLA_KP - GLA_KW))).astype(bf16),
        w_out=wo.astype(bf16))
    ffn1 = (ffn1_norm_g.astype(f32)[:, None, :], ffn1_w1.astype(bf16), ffn1_w3.astype(bf16), ffn1_w2.astype(bf16))
    ffn2 = (ffn2_norm_g.astype(f32)[:, None, :], ffn2_w1.astype(bf16), ffn2_w3.astype(bf16), ffn2_w2.astype(bf16))
    final_g = final_norm_g.astype(f32)[None, :]

    def run_trunk(x3d, states, *, sb, t_len, tm, need_output):
        n_seq, length, _ = x3d.shape
        consts = _mix_consts(t_len)
        new_states = ()
        for l in range(DEPTH):
            x2d = _ffn(x3d.reshape(n_seq * length, D_MODEL), *ffn1, final_g, layer=l, tm=tm, post_norm=False)
            x3d, new_states = _mix(x2d.reshape(n_seq, length, D_MODEL), states, new_states, lw, consts,
                                   layer=l, sb=sb, t_len=t_len)
            last = l == DEPTH - 1
            if need_output or not last:
                x2d = _ffn(x3d.reshape(n_seq * length, D_MODEL), *ffn2, final_g, layer=l, tm=tm, post_norm=last)
                x3d = x2d.reshape(n_seq, length, D_MODEL)
        return x3d, new_states

    zeros = lambda *shape: jnp.zeros((DEPTH, n_prompt) + shape, f32)
    zero_states = [zeros(LRU_W), zeros(SUBLANES, LRU_W), zeros(SUBLANES, HW), zeros(ML_H, ML_D, ML_D),
                   zeros(ML_H, ML_D), zeros(1, LANES), zeros(GLA_H, GLA_DK, GLA_DV)]
    x_meta = jnp.broadcast_to(meta_tokens.astype(f32)[None], (n_prompt, N_META, D_MODEL))
    _, meta_states = run_trunk(x_meta, zero_states, sb=n_prompt, t_len=N_META, tm=n_prompt * N_META,
                               need_output=False)

    y_prompt, p_states = run_trunk(x_prompt, meta_states, sb=n_prompt, t_len=CHUNK, tm=512, need_output=True)

    s_states = [state_lru_h.astype(f32), _tail(state_lru_conv), _tail(state_mlstm_conv, ML_D, HP),
                state_mlstm_C.astype(f32), state_mlstm_n.astype(f32),
                _pad_last(state_mlstm_m.astype(f32), LANES)[:, :, None, :], state_gla_S.astype(f32)]
    y_sample, s_states = run_trunk(x_sample, s_states, sb=16, t_len=t_sample, tm=512, need_output=True)

    def reference_layout(st):
        lru_h, lru_tail, ml_tail, ml_c, ml_n, ml_m, gla_s = st
        return (lru_h, _untail(lru_tail), ml_c, ml_n, ml_m[:, :, 0, :ML_H], _untail(ml_tail, ML_D, HP),
                gla_s)

    return (y_prompt, y_sample) + reference_layout(p_states) + reference_layout(s_states)
```

```python
import functools

import jax
import jax.numpy as jnp
from jax import lax
from jax.experimental import pallas as pl
from jax.experimental.pallas import tpu as pltpu

f32 = jnp.float32
bf16 = jnp.bfloat16

D_MODEL = 1024
DEPTH = 2
N_META = 16
CONV_W = 4
CHUNK = 64
LRU_W = 256
LRU_BLOCKS = 4
LRU_C = 8.0
ML_H = 4
ML_D = 96
ML_W = ML_H * ML_D
GLA_H = 4
GLA_DK = 48
GLA_DV = 96
GLA_KW = GLA_H * GLA_DK
GLA_VW = GLA_H * GLA_DV
GLA_RANK = 16
GLA_TAU = 16.0
D_FF = 2816
EPS = 1e-6

SUBLANES = 8
LANES = 128
BF16_SUBLANES = 16
VMEM_LIMIT_BYTES = 60 * 1024 * 1024

FFN_ROWS = 512
SAMPLE_SEQS = 16
FF_TILE = 256
N_FF_TILES = D_FF // FF_TILE
GLA_SUB = 16
INTERLEAVE_PARTS = 4

HP = LANES
HW = ML_H * HP
ONE_LANE = ML_D
GLA_KP = 2 * LANES

Z_UR = 0
Z_GR = Z_UR + LRU_W
Z_UM = Z_GR + LRU_W
Z_ZM = Z_UM + HW
Z_QG = Z_ZM + HW
Z_KG = Z_QG + GLA_KP
Z_VG = Z_KG + GLA_KP
Z_GG = Z_VG + HW
Z_AL = Z_GG + HW
Z_W = Z_AL + LANES


def _dot(a, b):
    return jnp.dot(a, b, preferred_element_type=f32)


def _dot_nt(a, b):
    return lax.dot_general(a, b, (((1,), (1,)), ((), ())), preferred_element_type=f32)


def _dot_tn(a, b):
    return lax.dot_general(a, b, (((0,), (0,)), ((), ())), preferred_element_type=f32)


def _rms(x, g):
    return x * lax.rsqrt(jnp.mean(x * x, axis=-1, keepdims=True) + EPS) * g


def _ffn_body(x_ref, g_ref, w1_ref, w3_ref, w2_ref, gf_ref, o_ref, gated_ref, *, post_norm):
    x = x_ref[...]
    h = _rms(x, g_ref[...]).astype(bf16)
    for f in range(N_FF_TILES):
        cols = slice(f * FF_TILE, (f + 1) * FF_TILE)
        a = _dot(h, w1_ref[:, cols])
        b = _dot(h, w3_ref[:, cols])
        gated_ref[:, cols] = (a * jax.nn.sigmoid(a) * b).astype(bf16)
    y = x + 0.5 * _dot(gated_ref[...], w2_ref[...])
    if post_norm:
        y = _rms(y, gf_ref[...])
    o_ref[...] = y


def _const_spec(shape):
    zeros = (0,) * len(shape)
    return pl.BlockSpec(shape, lambda *_: zeros, pipeline_mode=pl.Buffered(1))


def _layer_spec(shape, layer):
    zeros = (0,) * (len(shape) - 1)
    return pl.BlockSpec((None,) + tuple(shape[1:]), lambda *_: (layer,) + zeros, pipeline_mode=pl.Buffered(1))


def _ffn(x2d, g, w1, w3, w2, gf, *, layer, tm, post_norm):
    rows = x2d.shape[0]
    return pl.pallas_call(
        functools.partial(_ffn_body, post_norm=post_norm),
        out_shape=jax.ShapeDtypeStruct(x2d.shape, f32),
        grid=(rows // tm,),
        in_specs=[
            pl.BlockSpec((tm, D_MODEL), lambda i: (i, 0)),
            _layer_spec(g.shape, layer),
            _layer_spec(w1.shape, layer),
            _layer_spec(w3.shape, layer),
            _layer_spec(w2.shape, layer),
            _const_spec(gf.shape),
        ],
        out_specs=pl.BlockSpec((tm, D_MODEL), lambda i: (i, 0)),
        scratch_shapes=[pltpu.VMEM((tm, D_FF), bf16)],
        compiler_params=pltpu.CompilerParams(
            dimension_semantics=("arbitrary",), vmem_limit_bytes=VMEM_LIMIT_BYTES),
        name="ffn",
    )(x2d, g, w1, w3, w2, gf)


def _row_time(shape, period):
    return lax.broadcasted_iota(jnp.int32, shape, 0) % period


def _seg_cumsum(x, period):
    t = _row_time(x.shape, period)
    s = 1
    while s < period:
        x = x + jnp.where(t >= s, pltpu.roll(x, s, 0), 0.0)
        s *= 2
    return x


def _rows_from_seq(v, t_len):
    sb, width = v.shape
    return jnp.broadcast_to(v[:, None, :], (sb, t_len, width)).reshape(sb * t_len, width)


def _last_rows(x, groups, period):
    return x.reshape(groups, period, x.shape[-1])[:, period - 1, :]


def _key_head(idx):
    head = jnp.zeros_like(idx)
    for h in range(1, GLA_H + 1):
        head = head + (idx >= h * GLA_DK).astype(jnp.int32)
    return head


def _expand(x, e2):
    hi = x.astype(bf16)
    lo = (x - hi.astype(f32)).astype(bf16)
    return _dot(jnp.concatenate([hi, lo], axis=1), e2)


def _causal_conv(ext_ref, u, w, b, sb, t_len):
    width = u.shape[-1]
    ext_ref[:, SUBLANES:SUBLANES + t_len, :] = u.reshape(sb, t_len, width)
    base = SUBLANES - (CONV_W - 1)
    out = b + ext_ref[:, base:base + t_len, :] * w[0:1, :]
    for j in range(1, CONV_W):
        out = out + ext_ref[:, base + j:base + j + t_len, :] * w[j:j + 1, :]
    ext_ref[:, 0:SUBLANES, :] = ext_ref[:, t_len:t_len + SUBLANES, :]
    return out.reshape(sb * t_len, width)


V_MIX_G = 0
V_LRU_CW = 1
V_LRU_CB = 5
V_LRU_BA = 6
V_LRU_BX = 7
V_LRU_LAM = 8
V_ML_CW = 9
V_ML_CB = 13
V_ML_BIF = 14
V_ML_G = 15
V_ML_SKIP = 16
V_GLA_BUP = 17
V_GLA_G = 18
V_ONES = 19
VEC_ROWS = 24
N_STATES = 7
_MIX_WEIGHT_KEYS = ("vecs", "w_in", "lru_wa", "lru_wx", "ml_wq", "ml_wk", "ml_wv", "ml_wif", "gla_wup", "w_out")
N_MIX_CONSTS = 2
N_MIX_INPUTS = 1 + N_STATES + len(_MIX_WEIGHT_KEYS) + N_MIX_CONSTS


def _mix_body(*refs, sb, t_len):
    (x_ref, h0_ref, tr0_ref, tm0_ref, c0_ref, n0_ref, m0_ref, s0_ref,
     vec_ref, win_ref, wa_ref, wx_ref, wq_ref, wk_ref, wv_ref, wif_ref, wup_ref, wout_ref,
     e2_ref, ert2_ref) = refs[:N_MIX_INPUTS]
    (y_ref, h_ref, tro_ref, tmo_ref, c_ref, n_ref, m_ref, s_ref,
     extr, extm, z_ref, mq_ref, mk_ref, kw_ref, va_ref, rtx_ref, wix_ref, eix_ref, cc_ref,
     decx_ref, hm_ref, qt_ref, kh_ref, gdec_ref, abig_ref, og_ref, yr_ref, cm_ref,
     cst_ref, gst_ref) = refs[N_MIX_INPUTS + N_STATES:]

    def vec(row, width, n=1):
        return vec_ref[row:row + n, 0:width]

    chunk = pl.program_id(1)
    n_rows = sb * t_len
    eye_d = (lax.broadcasted_iota(jnp.int32, (ML_D, ML_D), 0)
             == lax.broadcasted_iota(jnp.int32, (ML_D, ML_D), 1))

    @pl.when(chunk == 0)
    def _load_state():
        h_ref[...] = h0_ref[...]
        extr[:, 0:SUBLANES, :] = tr0_ref[...]
        extm[:, 0:SUBLANES, :] = tm0_ref[...]
        m_ref[...] = m0_ref[...]

        def load(s, carry):
            cst_ref[s] = jnp.zeros((HP, HW), f32)
            gst_ref[s] = jnp.zeros((GLA_KP, HP), f32)
            for h in range(GLA_H):
                gst_ref[s, h * GLA_DK:(h + 1) * GLA_DK, 0:GLA_DV] = s0_ref[s, h]
            for h in range(ML_H):
                cst_ref[s, 0:ML_D, h * HP:h * HP + ML_D] = c0_ref[s, h]
                n_col = jnp.sum(jnp.where(eye_d, n0_ref[s, h:h + 1, :], 0.0), axis=1, keepdims=True)
                cst_ref[s, 0:ML_D, h * HP + ONE_LANE:h * HP + ONE_LANE + 1] = n_col
            return carry

        lax.fori_loop(0, sb, load, 0)

    x = x_ref[...].reshape(n_rows, D_MODEL)
    hn = _rms(x, vec(V_MIX_G, D_MODEL)).astype(bf16)
    z_ref[...] = _dot(hn, win_ref[...])

    xr = _causal_conv(extr, z_ref[:, Z_UR:Z_UR + LRU_W], vec(V_LRU_CW, LRU_W, CONV_W), vec(V_LRU_CB, LRU_W),
                      sb, t_len)
    xr_b = xr.astype(bf16)
    r = jax.nn.sigmoid(_dot(xr_b, wa_ref[...]) + vec(V_LRU_BA, LRU_W))
    ig = jax.nn.sigmoid(_dot(xr_b, wx_ref[...]) + vec(V_LRU_BX, LRU_W))
    log_a = -LRU_C * r * jax.nn.softplus(-vec(V_LRU_LAM, LRU_W))
    a = jnp.exp(log_a)
    one_m_a2 = 1.0 - a * a
    root = jnp.where(one_m_a2 > 0.0, one_m_a2 * lax.rsqrt(one_m_a2), 0.0)
    bt = root * (ig * xr)
    tl = _row_time((n_rows, LRU_W), t_len)
    s = 1
    while s < t_len:
        keep = tl >= s
        a_prev = jnp.where(keep, pltpu.roll(a, s, 0), 1.0)
        b_prev = jnp.where(keep, pltpu.roll(bt, s, 0), 0.0)
        bt = a * b_prev + bt
        a = a * a_prev
        s *= 2
    hs = bt + a * _rows_from_seq(h_ref[...], t_len)
    h_ref[...] = _last_rows(hs, sb, t_len)
    yr_ref[...] = jax.nn.gelu(z_ref[:, Z_GR:Z_GR + LRU_W]) * hs

    u_m = z_ref[:, Z_UM:Z_UM + HW]
    cm = jax.nn.silu(_causal_conv(extm, u_m, vec(V_ML_CW, HW, CONV_W), vec(V_ML_CB, HW), sb, t_len))
    cm_ref[...] = cm
    cm_b = cm.astype(bf16)
    mq = _dot(cm_b, wq_ref[...])
    mk = _dot(cm_b, wk_ref[...])
    va = _dot(u_m.astype(bf16), wv_ref[...]) + vec(V_ONES, HW)
    gates = _dot(jnp.concatenate([mq.astype(bf16), mk.astype(bf16), va.astype(bf16)], axis=1),
                 wif_ref[...]) + vec(V_ML_BIF, 2 * LANES)
    li = gates[:, 0:LANES]
    lf = jax.nn.log_sigmoid(gates[:, LANES:2 * LANES])
    tg = _row_time((n_rows, LANES), t_len)
    bcum, m_loc = lf, li
    s = 1
    while s < t_len:
        keep = tg >= s
        b_prev = jnp.where(keep, pltpu.roll(bcum, s, 0), 0.0)
        m_prev_seg = jnp.where(keep, pltpu.roll(m_loc, s, 0), -jnp.inf)
        m_loc = jnp.maximum(m_prev_seg + bcum, m_loc)
        bcum = bcum + b_prev
        s *= 2
    m_old = m_ref[:, 0, :]
    m_old_rows = _rows_from_seq(m_old, t_len)
    m_t = jnp.maximum(bcum + m_old_rows, m_loc)
    m_new = _last_rows(m_t, sb, t_len)
    b_last = _last_rows(bcum, sb, t_len)
    m_ref[...] = m_new[:, None, :]
    cc = li - bcum
    cc_ref[...] = cc
    e2 = e2_ref[...]
    rtx_ref[...] = _expand(bcum - m_t, ert2_ref[...])
    wix_ref[...] = jnp.exp(_expand(bcum + m_old_rows - m_t, e2))
    eix_ref[...] = jnp.exp(_expand(-m_t, e2))
    w_k = jnp.exp(_expand(_rows_from_seq(b_last, t_len) + cc - _rows_from_seq(m_new, t_len), e2))
    decx_ref[...] = jnp.exp(_expand(b_last + m_old - m_new, e2))[:, None, :]
    mq_ref[...] = (mq * (ML_D ** -0.5)).astype(mq_ref.dtype)
    mk_ref[...] = mk.astype(mk_ref.dtype)
    kw_ref[...] = (mk * w_k).astype(kw_ref.dtype)
    va_ref[...] = va.astype(va_ref.dtype)

    sub = min(GLA_SUB, t_len)
    n_sub = t_len // sub
    al_b = z_ref[:, Z_AL:Z_AL + LANES].astype(bf16)
    lg = jax.nn.log_sigmoid(_dot(al_b, wup_ref[...]) + vec(V_GLA_BUP, GLA_KP)) * (1.0 / GLA_TAU)
    bcl = _seg_cumsum(lg, sub)
    gq = z_ref[:, Z_QG:Z_QG + GLA_KP] * (GLA_DK ** -0.5)
    gk = z_ref[:, Z_KG:Z_KG + GLA_KP]
    last = _last_rows(bcl, n_rows // sub, sub)
    dec = jnp.exp(last).reshape(sb, n_sub, GLA_KP)
    gdec_ref[...] = jnp.concatenate([dec, jnp.ones((sb, SUBLANES - n_sub, GLA_KP), f32)],
                                    axis=1).reshape(sb * SUBLANES, GLA_KP)
    qt_ref[...] = gq * jnp.exp(bcl)
    kh_ref[...] = gk * jnp.exp(_rows_from_seq(last, sub) - bcl)
    n_grp = n_rows // sub

    def grp_row(arr, i):
        picked = arr.reshape(n_grp, sub, arr.shape[-1])[:, i, :]
        return jnp.broadcast_to(picked[:, None, :], (n_grp, sub, arr.shape[-1])).reshape(arr.shape)

    ts = _row_time((n_rows, GLA_KP), sub)
    key_head_r = _key_head(lax.broadcasted_iota(jnp.int32, (GLA_KP, LANES), 0))
    out_lane = lax.broadcasted_iota(jnp.int32, (GLA_KP, LANES), 1)
    out_src = jnp.where(key_head_r == out_lane // sub, out_lane % sub, -1)

    t4 = ML_H * t_len
    causal_cat = (lax.broadcasted_iota(jnp.int32, (t_len, t4), 0)
                  >= lax.broadcasted_iota(jnp.int32, (t_len, t4), 1) % t_len)
    stack_diag = (lax.broadcasted_iota(jnp.int32, (t4, HW), 0) // t_len
                  == lax.broadcasted_iota(jnp.int32, (t4, HW), 1) // HP)
    state_diag = (lax.broadcasted_iota(jnp.int32, (HW, HW), 0) // HP
                  == lax.broadcasted_iota(jnp.int32, (HW, HW), 1) // HP)
    lane_head = lax.broadcasted_iota(jnp.int32, (t_len, HW), 1) // HP

    def mlstm_seq(s):
        rows = pl.ds(s * t_len, t_len)
        q_b = mq_ref[rows, :].astype(bf16)
        k_bd = jnp.where(stack_diag, jnp.concatenate([mk_ref[rows, :]] * ML_H, axis=0), 0.0).astype(bf16)
        v_bd = jnp.where(stack_diag, jnp.concatenate([va_ref[rows, :]] * ML_H, axis=0), 0.0).astype(bf16)
        c_t = cc_ref[rows, :].T
        c_row = jnp.concatenate([c_t[h:h + 1, :] for h in range(ML_H)], axis=1)
        dmat = jnp.where(causal_cat, rtx_ref[rows, :] + c_row, -jnp.inf)
        sc = _dot_nt(q_b, k_bd) * jnp.exp(dmat)
        c_old = cst_ref[s]
        c_bd = jnp.where(state_diag, jnp.concatenate([c_old.astype(bf16)] * ML_H, axis=0), 0.0)
        num = _dot(sc.astype(bf16), v_bd) + _dot(q_b, c_bd) * wix_ref[rows, :]
        den = jnp.zeros((t_len, HW), f32)
        for h in range(ML_H):
            den = jnp.where(lane_head == h, num[:, h * HP + ONE_LANE:h * HP + ONE_LANE + 1], den)
        hm_ref[rows, :] = num / jnp.maximum(jnp.abs(den), eix_ref[rows, :])
        kw_st = jnp.concatenate([kw_ref[rows, h * HP:(h + 1) * HP] for h in range(ML_H)],
                                axis=0).astype(bf16)
        cst_ref[s] = c_old * decx_ref[s] + _dot_tn(kw_st, v_bd)

    a_cat = jnp.zeros((n_rows, LANES), f32)
    for part in range(INTERLEAVE_PARTS):
        for i in range(part * sub // INTERLEAVE_PARTS, (part + 1) * sub // INTERLEAVE_PARTS):
            e = jnp.where(ts >= i, gq * grp_row(gk, i) * jnp.exp(bcl - grp_row(bcl, i)), 0.0)
            a_cat = a_cat + _dot(e.astype(bf16), jnp.where(out_src == i, 1.0, 0.0).astype(bf16))
        for s in range(part * sb // INTERLEAVE_PARTS, (part + 1) * sb // INTERLEAVE_PARTS):
            mlstm_seq(s)
    ex_r = lax.broadcasted_iota(jnp.int32, (LANES, t4), 0)
    ex_l = lax.broadcasted_iota(jnp.int32, (LANES, t4), 1)
    ex = jnp.where(ex_r // sub == ex_l // t_len, jnp.where(ex_r % sub == ex_l % sub, 1.0, 0.0), 0.0).astype(bf16)
    row_grp = (lax.broadcasted_iota(jnp.int32, (n_rows, t4), 0) % t_len) // sub
    lane_grp = (lax.broadcasted_iota(jnp.int32, (n_rows, t4), 1) % t_len) // sub
    abig_ref[...] = jnp.where(row_grp == lane_grp, _dot(a_cat.astype(bf16), ex), 0.0)

    stack_head = lax.broadcasted_iota(jnp.int32, (GLA_H * sub, GLA_KP), 0) // sub
    key_head = _key_head(lax.broadcasted_iota(jnp.int32, (GLA_H * sub, GLA_KP), 1))
    head_diag = stack_head == key_head

    def gla_seq(s):
        st = gst_ref[s]
        dec_t = gdec_ref[pl.ds(s * SUBLANES, SUBLANES), :].T
        rows_t = pl.ds(s * t_len, t_len)
        gv_bd = jnp.where(stack_diag, jnp.concatenate([z_ref[rows_t, Z_VG:Z_VG + HW]] * GLA_H, axis=0),
                          0.0).astype(bf16)
        og_ref[rows_t, :] = _dot(abig_ref[rows_t, :].astype(bf16), gv_bd)
        for j in range(n_sub):
            rows = pl.ds(s * t_len + j * sub, sub)
            q_bd =jnp.where(head_diag, jnp.concatenate([qt_ref[rows, :]] * GLA_H, axis=0), 0.0).astype(bf16)
            k_bd = jnp.where(head_diag, jnp.concatenate([kh_ref[rows, :]] * GLA_H, axis=0), 0.0).astype(bf16)
            v_st = jnp.concatenate([z_ref[rows, Z_VG + h * HP:Z_VG + (h + 1) * HP] for h in range(GLA_H)],
                                   axis=0).astype(bf16)
            o4 = _dot(q_bd, st.astype(bf16))
            for h in range(GLA_H):
                lanes = slice(h * HP, (h + 1) * HP)
                og_ref[rows, lanes] = og_ref[rows, lanes] + o4[h * sub:(h + 1) * sub, :]
            st = st * dec_t[:, j:j + 1] + _dot_tn(k_bd, v_st)
        gst_ref[s] = st

    for s in range(sb):
        gla_seq(s)

    def head_norm(o):
        real = lax.broadcasted_iota(jnp.int32, (n_rows, HP), 1) < ML_D
        normed = []
        for h in range(ML_H):
            seg = jnp.where(real, o[:, h * HP:(h + 1) * HP], 0.0)
            ms = jnp.sum(seg * seg, axis=-1, keepdims=True) * (1.0 / ML_D)
            normed.append(seg * lax.rsqrt(ms + EPS))
        return jnp.concatenate(normed, axis=1)

    y_m = (jax.nn.sigmoid(z_ref[:, Z_ZM:Z_ZM + HW])
           * (head_norm(hm_ref[...]) * vec(V_ML_G, HW) + vec(V_ML_SKIP, HW) * cm_ref[...]))
    y_g = head_norm(og_ref[...]) * vec(V_GLA_G, HW) * jax.nn.silu(z_ref[:, Z_GG:Z_GG + HW])
    y = (_dot(yr_ref[...].astype(bf16), wout_ref[0:LRU_W, :])
         + _dot(y_m.astype(bf16), wout_ref[LRU_W:LRU_W + HW, :])
         + _dot(y_g.astype(bf16), wout_ref[LRU_W + HW:, :]))
    y_ref[...] = x_ref[...] + y.reshape(sb, t_len, D_MODEL)

    @pl.when(chunk == pl.num_programs(1) - 1)
    def _store_state():
        tro_ref[...] = extr[:, 0:SUBLANES, :]
        tmo_ref[...] = extm[:, 0:SUBLANES, :]

        def store(s, carry):
            for h in range(GLA_H):
                s_ref[s, h] = gst_ref[s, h * GLA_DK:(h + 1) * GLA_DK, 0:GLA_DV]
            for h in range(ML_H):
                c_ref[s, h] = cst_ref[s, 0:ML_D, h * HP:h * HP + ML_D]
                n_col = cst_ref[s, 0:ML_D, h * HP + ONE_LANE:h * HP + ONE_LANE + 1]
                n_ref[s, h:h + 1, :] = jnp.sum(jnp.where(eye_d, n_col, 0.0), axis=0, keepdims=True)
            return carry

        lax.fori_loop(0, sb, store, 0)


def _mix(x3d, states, donors, lw, consts, *, layer, sb, t_len):
    n_seq, length, _ = x3d.shape
    n_rows = sb * t_len
    n_sub = t_len // min(GLA_SUB, t_len)
    grid = (n_seq // sb, length // t_len)

    def seq_spec(arr):
        blk = (None, sb) + arr.shape[2:]
        zeros = (0,) * (arr.ndim - 2)
        return pl.BlockSpec(blk, lambda i, c: (layer, i) + zeros, pipeline_mode=pl.Buffered(1))

    weights = [lw[k] for k in _MIX_WEIGHT_KEYS]
    assert len(states) == len(donors) == N_STATES and len(consts) == N_MIX_CONSTS
    x_spec = pl.BlockSpec((sb, t_len, D_MODEL), lambda i, c: (i, c, 0))
    row_scratch = lambda width, dtype=f32: pltpu.VMEM((n_rows, width), dtype)
    opnd = bf16 if t_len % BF16_SUBLANES == 0 else f32
    outs = pl.pallas_call(
        functools.partial(_mix_body, sb=sb, t_len=t_len),
        out_shape=[jax.ShapeDtypeStruct(x3d.shape, f32)] + [jax.ShapeDtypeStruct(a.shape, f32) for a in states],
        grid=grid,
        in_specs=([x_spec] + [seq_spec(a) for a in states] + [_layer_spec(w.shape, layer) for w in weights]
                  + [_const_spec(c.shape) for c in consts] + [pl.BlockSpec(memory_space=pl.ANY) for _ in donors]),
        out_specs=[x_spec] + [seq_spec(a) for a in states],
        input_output_aliases={N_MIX_INPUTS + k: 1 + k for k in range(N_STATES)},
        scratch_shapes=[
            pltpu.VMEM((sb, t_len + SUBLANES, LRU_W), f32),
            pltpu.VMEM((sb, t_len + SUBLANES, HW), f32),
            row_scratch(Z_W),
            row_scratch(HW, opnd),
            row_scratch(HW, opnd),
            row_scratch(HW, opnd),
            row_scratch(HW, opnd),
            row_scratch(ML_H * t_len),
            row_scratch(HW),
            row_scratch(HW),
            row_scratch(LANES),
            pltpu.VMEM((sb, 1, HW), f32),
            row_scratch(HW),
            row_scratch(GLA_KP),
            row_scratch(GLA_KP),
            pltpu.VMEM((sb * SUBLANES, GLA_KP), f32),
            row_scratch(ML_H * t_len),
            row_scratch(HW),
            row_scratch(LRU_W),
            row_scratch(HW),
            pltpu.VMEM((sb, HP, HW), f32),
            pltpu.VMEM((sb, GLA_KP, HP), f32),
        ],
        compiler_params=pltpu.CompilerParams(
            dimension_semantics=("arbitrary", "arbitrary"), vmem_limit_bytes=VMEM_LIMIT_BYTES),
        name="mixer",
    )(x3d, *states, *weights, *consts, *donors)
    return outs[0], list(outs[1:])


def _pad_last(w, width):
    return jnp.pad(w, [(0, 0)] * (w.ndim - 1) + [(0, width - w.shape[-1])])


def _pad_heads(w, d, dp):
    lead = w.shape[:-1]
    w = _pad_last(w.reshape(lead + (w.shape[-1] // d, d)), dp)
    return w.reshape(lead + (-1,))


def _pad_head_rows(w, d, dp):
    return jnp.swapaxes(_pad_heads(jnp.swapaxes(w, -1, -2), d, dp), -1, -2)


def _block_diag(blocks, dp=None):
    depth, n, d, _ = blocks.shape
    dp = d if dp is None else dp
    blocks = jnp.pad(blocks, ((0, 0), (0, 0), (0, dp - d), (0, dp - d)))
    eye = jnp.eye(n, dtype=blocks.dtype)
    return jnp.einsum("lhij,hg->lhigj", blocks, eye).reshape(depth, n * dp, n * dp)


def _vec_rows(v):
    v = v.astype(f32)
    v = v[:, None, :] if v.ndim == 2 else v
    return _pad_last(v, D_MODEL)


def _tail(conv_state, d=None, dp=None):
    conv_state = conv_state.astype(f32)
    if d is not None:
        conv_state = _pad_heads(conv_state, d, dp)
    return jnp.pad(conv_state, ((0, 0), (0, 0), (SUBLANES - (CONV_W - 1), 0), (0, 0)))


def _untail(tail, d=None, dp=None):
    tail = tail[:, :, SUBLANES - (CONV_W - 1):, :]
    if d is not None:
        tail = tail.reshape(tail.shape[:3] + (-1, dp))[..., :d].reshape(tail.shape[:3] + (-1,))
    return tail


def _mix_consts(t_len):
    head = jnp.arange(LANES)[:, None]
    lane = jnp.arange(HW)
    e = (head == (lane // HP)[None, :]).astype(bf16)
    e_rt = (head == (jnp.arange(ML_H * t_len) // t_len)[None, :]).astype(bf16)
    return jnp.concatenate([e, e], axis=0), jnp.concatenate([e_rt, e_rt], axis=0)


def kernel(x_prompt, x_sample, state_lru_h, state_lru_conv, state_mlstm_C, state_mlstm_n, state_mlstm_m, state_mlstm_conv, state_gla_S, meta_tokens, ffn1_norm_g, ffn1_w1, ffn1_w3, ffn1_w2, mix_norm_g, w_in, lru_conv_w, lru_conv_b, lru_wa, lru_ba, lru_wx, lru_bx, lru_lambda, ml_conv_w, ml_conv_b, ml_wq, ml_wk, ml_wv, ml_w_if, ml_b_if, ml_norm_g, ml_skip, gla_w_up, gla_b_up, gla_norm_g, w_out, ffn2_norm_g, ffn2_w1, ffn2_w3, ffn2_w2, final_norm_g):
    n_prompt = x_prompt.shape[0]
    n_sample, t_sample, _ = x_sample.shape

    sizes = (LRU_W, LRU_W, ML_W, ML_W, GLA_KW, GLA_KW, GLA_VW, GLA_VW, GLA_RANK)
    widths = (LRU_W, LRU_W, HW, HW, GLA_KP, GLA_KP, HW, HW, LANES)
    head_pads = (None, None, (ML_D, HP), (ML_D, HP), None, None, (GLA_DV, HP), (GLA_DV, HP), None)
    parts, off = [], 0
    for size, width, pad in zip(sizes, widths, head_pads):
        part = w_in[..., off:off + size]
        parts.append(_pad_last(part, width) if pad is None else _pad_heads(part, *pad))
        off += size
    wif = jnp.concatenate([_pad_head_rows(ml_w_if[:, i * ML_W:(i + 1) * ML_W], ML_D, HP) for i in range(3)],
                          axis=1)
    wif = jnp.concatenate([_pad_last(wif[..., :ML_H], LANES), _pad_last(wif[..., ML_H:], LANES)], axis=-1)
    bif = jnp.concatenate([_pad_last(ml_b_if[:, :ML_H], LANES), _pad_last(ml_b_if[:, ML_H:], LANES)], axis=-1)
    wo = jnp.concatenate([w_out[:, :LRU_W], _pad_head_rows(w_out[:, LRU_W:LRU_W + ML_W], ML_D, HP),
                          _pad_head_rows(w_out[:, LRU_W + ML_W:], GLA_DV, HP)], axis=1)
    ones_row = jnp.broadcast_to(((jnp.arange(HW) % HP) == ONE_LANE).astype(f32), (DEPTH, HW))
    vec_rows = [mix_norm_g, lru_conv_w, lru_conv_b, lru_ba, lru_bx, lru_lambda,
                _pad_heads(ml_conv_w, ML_D, HP), _pad_heads(ml_conv_b, ML_D, HP), bif,
                _pad_heads(ml_norm_g, ML_D, HP), _pad_heads(ml_skip, ML_D, HP),
                gla_b_up, _pad_heads(gla_norm_g, GLA_DV, HP), ones_row]
    vecs = jnp.concatenate([_vec_rows(v) for v in vec_rows], axis=1)
    vecs = jnp.pad(vecs, ((0, 0), (0, VEC_ROWS - vecs.shape[1]), (0, 0)))
    lw = dict(
        vecs=vecs,
        w_in=jnp.concatenate(parts, axis=-1).astype(bf16),
        lru_wa=_block_diag(lru_wa).astype(bf16), lru_wx=_block_diag(lru_wx).astype(bf16),
        ml_wq=_block_diag(ml_wq, HP).astype(bf16), ml_wk=_block_diag(ml_wk, HP).astype(bf16),
        ml_wv=_block_diag(ml_wv, HP).astype(bf16), ml_wif=wif.astype(bf16),
        gla_wup=jnp.pad(gla_w_up, ((0, 0), (0, LANES - GLA_RANK), (0, GLA_KP - GLA_KW))).astype(bf16),
        w_out=wo.astype(bf16))
    ffn1 = (ffn1_norm_g.astype(f32)[:, None, :], ffn1_w1.astype(bf16), ffn1_w3.astype(bf16), ffn1_w2.astype(bf16))
    ffn2 = (ffn2_norm_g.astype(f32)[:, None, :], ffn2_w1.astype(bf16), ffn2_w3.astype(bf16), ffn2_w2.astype(bf16))
    final_g = final_norm_g.astype(f32)[None, :]

    def run_trunk(x3d, states, *, sb, t_len, tm, need_output):
        n_seq, length, _ = x3d.shape
        consts = _mix_consts(t_len)
        new_states = [jnp.zeros(a.shape, f32) for a in states]
        for l in range(DEPTH):
            x2d = _ffn(x3d.reshape(n_seq * length, D_MODEL), *ffn1, final_g, layer=l, tm=tm, post_norm=False)
            x3d, new_states = _mix(x2d.reshape(n_seq, length, D_MODEL), states, new_states, lw, consts,
                                   layer=l, sb=sb, t_len=t_len)
            last = l == DEPTH - 1
            if need_output or not last:
                x2d = _ffn(x3d.reshape(n_seq * length, D_MODEL), *ffn2, final_g, layer=l, tm=tm, post_norm=last)
                x3d = x2d.reshape(n_seq, length, D_MODEL)
        return x3d, new_states

    zeros = lambda *shape: jnp.zeros((DEPTH, n_prompt) + shape, f32)
    zero_states = [zeros(LRU_W), zeros(SUBLANES, LRU_W), zeros(SUBLANES, HW), zeros(ML_H, ML_D, ML_D),
                   zeros(ML_H, ML_D), zeros(1, LANES), zeros(GLA_H, GLA_DK, GLA_DV)]
    x_meta = jnp.broadcast_to(meta_tokens.astype(f32)[None], (n_prompt, N_META, D_MODEL))
    _, meta_states = run_trunk(x_meta, zero_states, sb=n_prompt, t_len=N_META, tm=n_prompt * N_META,
                               need_output=False)

    y_prompt, p_states = run_trunk(x_prompt, meta_states, sb=n_prompt, t_len=CHUNK, tm=FFN_ROWS, need_output=True)

    s_states = [state_lru_h.astype(f32), _tail(state_lru_conv), _tail(state_mlstm_conv, ML_D, HP),
                state_mlstm_C.astype(f32), state_mlstm_n.astype(f32),
                _pad_last(state_mlstm_m.astype(f32), LANES)[:, :, None, :], state_gla_S.astype(f32)]
    y_sample, s_states = run_trunk(x_sample, s_states, sb=SAMPLE_SEQS, t_len=t_sample, tm=FFN_ROWS,
                                   need_output=True)

    def reference_layout(st):
        lru_h, lru_tail, ml_tail, ml_c, ml_n, ml_m, gla_s = st
        return (lru_h, _untail(lru_tail), ml_c, ml_n, ml_m[:, :, 0, :ML_H], _untail(ml_tail, ML_D, HP),
                gla_s)

    return (y_prompt, y_sample) + reference_layout(p_states) + reference_layout(s_states)
```

```python
import functools

import jax
import jax.numpy as jnp
from jax import lax
from jax.experimental import pallas as pl
from jax.experimental.pallas import tpu as pltpu

f32 = jnp.float32
bf16 = jnp.bfloat16

D_MODEL = 1024
DEPTH = 2
N_META = 16
CONV_W = 4
CHUNK = 64
LRU_W = 256
LRU_BLOCKS = 4
LRU_C = 8.0
ML_H = 4
ML_D = 96
ML_W = ML_H * ML_D
GLA_H = 4
GLA_DK = 48
GLA_DV = 96
GLA_KW = GLA_H * GLA_DK
GLA_VW = GLA_H * GLA_DV
GLA_RANK = 16
GLA_TAU = 16.0
D_FF = 2816
EPS = 1e-6

SUBLANES = 8
LANES = 128
BF16_SUBLANES = 16
VMEM_LIMIT_BYTES = 60 * 1024 * 1024

FFN_ROWS = 512
SAMPLE_SEQS = 16
FF_TILE = 256
N_FF_TILES = D_FF // FF_TILE
GLA_SUB = 16
INTERLEAVE_PARTS = 4

HP = LANES
HW = ML_H * HP
ONE_LANE = ML_D
GLA_KP = 2 * LANES

Z_UR = 0
Z_GR = Z_UR + LRU_W
Z_UM = Z_GR + LRU_W
Z_ZM = Z_UM + HW
Z_QG = Z_ZM + HW
Z_KG = Z_QG + GLA_KP
Z_VG = Z_KG + GLA_KP
Z_GG = Z_VG + HW
Z_AL = Z_GG + HW
Z_W = Z_AL + LANES


def _dot(a, b):
    return jnp.dot(a, b, preferred_element_type=f32)


def _dot_nt(a, b):
    return lax.dot_general(a, b, (((1,), (1,)), ((), ())), preferred_element_type=f32)


def _dot_tn(a, b):
    return lax.dot_general(a, b, (((0,), (0,)), ((), ())), preferred_element_type=f32)


def _rms(x, g):
    return x * lax.rsqrt(jnp.mean(x * x, axis=-1, keepdims=True) + EPS) * g


def _ffn_body(x_ref, g_ref, w1_ref, w3_ref, w2_ref, gf_ref, o_ref, gated_ref, *, post_norm):
    x = x_ref[...]
    h = _rms(x, g_ref[...]).astype(bf16)
    for f in range(N_FF_TILES):
        cols = slice(f * FF_TILE, (f + 1) * FF_TILE)
        a = _dot(h, w1_ref[:, cols])
        b = _dot(h, w3_ref[:, cols])
        gated_ref[:, cols] = (a * jax.nn.sigmoid(a) * b).astype(bf16)
    y = x + 0.5 * _dot(gated_ref[...], w2_ref[...])
    if post_norm:
        y = _rms(y, gf_ref[...])
    o_ref[...] = y


def _const_spec(shape):
    zeros = (0,) * len(shape)
    return pl.BlockSpec(shape, lambda *_: zeros, pipeline_mode=pl.Buffered(1))


def _layer_spec(shape, layer):
    zeros = (0,) * (len(shape) - 1)
    return pl.BlockSpec((None,) + tuple(shape[1:]), lambda *_: (layer,) + zeros, pipeline_mode=pl.Buffered(1))


def _ffn(x2d, g, w1, w3, w2, gf, *, layer, tm, post_norm):
    rows = x2d.shape[0]
    return pl.pallas_call(
        functools.partial(_ffn_body, post_norm=post_norm),
        out_shape=jax.ShapeDtypeStruct(x2d.shape, f32),
        grid=(rows // tm,),
        in_specs=[
            pl.BlockSpec((tm, D_MODEL), lambda i: (i, 0)),
            _layer_spec(g.shape, layer),
            _layer_spec(w1.shape, layer),
            _layer_spec(w3.shape, layer),
            _layer_spec(w2.shape, layer),
            _const_spec(gf.shape),
        ],
        out_specs=pl.BlockSpec((tm, D_MODEL), lambda i: (i, 0)),
        scratch_shapes=[pltpu.VMEM((tm, D_FF), bf16)],
        compiler_params=pltpu.CompilerParams(
            dimension_semantics=("arbitrary",), vmem_limit_bytes=VMEM_LIMIT_BYTES),
        name="ffn",
    )(x2d, g, w1, w3, w2, gf)


def _row_time(shape, period):
    return lax.broadcasted_iota(jnp.int32, shape, 0) % period


def _seg_cumsum(x, period):
    t = _row_time(x.shape, period)
    s = 1
    while s < period:
        x = x + jnp.where(t >= s, pltpu.roll(x, s, 0), 0.0)
        s *= 2
    return x


def _rows_from_seq(v, t_len):
    sb, width = v.shape
    return jnp.broadcast_to(v[:, None, :], (sb, t_len, width)).reshape(sb * t_len, width)


def _last_rows(x, groups, period):
    return x.reshape(groups, period, x.shape[-1])[:, period - 1, :]


def _key_head(idx):
    head = jnp.zeros_like(idx)
    for h in range(1, GLA_H + 1):
        head = head + (idx >= h * GLA_DK).astype(jnp.int32)
    return head


def _expand(x, e2):
    hi = x.astype(bf16)
    lo = (x - hi.astype(f32)).astype(bf16)
    return _dot(jnp.concatenate([hi, lo], axis=1), e2)


def _causal_conv(ext_ref, u, w, b, sb, t_len):
    width = u.shape[-1]
    ext_ref[:, SUBLANES:SUBLANES + t_len, :] = u.reshape(sb, t_len, width)
    base = SUBLANES - (CONV_W - 1)
    out = b + ext_ref[:, base:base + t_len, :] * w[0:1, :]
    for j in range(1, CONV_W):
        out = out + ext_ref[:, base + j:base + j + t_len, :] * w[j:j + 1, :]
    ext_ref[:, 0:SUBLANES, :] = ext_ref[:, t_len:t_len + SUBLANES, :]
    return out.reshape(sb * t_len, width)


V_MIX_G = 0
V_LRU_CW = 1
V_LRU_CB = 5
V_LRU_BA = 6
V_LRU_BX = 7
V_LRU_LAM = 8
V_ML_CW = 9
V_ML_CB = 13
V_ML_BIF = 14
V_ML_G = 15
V_ML_SKIP = 16
V_GLA_BUP = 17
V_GLA_G = 18
V_ONES = 19
VEC_ROWS = 24
N_STATES = 7
_MIX_WEIGHT_KEYS = ("vecs", "w_in", "lru_wa", "lru_wx", "ml_wq", "ml_wk", "ml_wv", "ml_wif", "gla_wup", "w_out")
N_MIX_CONSTS = 2
N_MIX_INPUTS = 1 + N_STATES + len(_MIX_WEIGHT_KEYS) + N_MIX_CONSTS


def _mix_body(*refs, sb, t_len):
    (x_ref, h0_ref, tr0_ref, tm0_ref, c0_ref, n0_ref, m0_ref, s0_ref,
     vec_ref, win_ref, wa_ref, wx_ref, wq_ref, wk_ref, wv_ref, wif_ref, wup_ref, wout_ref,
     e2_ref, ert2_ref) = refs[:N_MIX_INPUTS]
    (y_ref, h_ref, tro_ref, tmo_ref, c_ref, n_ref, m_ref, s_ref,
     extr, extm, z_ref, mq_ref, mk_ref, kw_ref, va_ref, rtx_ref, wix_ref, eix_ref, cc_ref,
     decx_ref, hm_ref, qt_ref, kh_ref, gdec_ref, abig_ref, og_ref, yr_ref, cm_ref,
     cst_ref, gst_ref) = refs[N_MIX_INPUTS:]

    def vec(row, width, n=1):
        return vec_ref[row:row + n, 0:width]

    chunk = pl.program_id(1)
    n_rows = sb * t_len
    eye_d = (lax.broadcasted_iota(jnp.int32, (ML_D, ML_D), 0)
             == lax.broadcasted_iota(jnp.int32, (ML_D, ML_D), 1))

    @pl.when(chunk == 0)
    def _load_state():
        h_ref[...] = h0_ref[...]
        extr[:, 0:SUBLANES, :] = tr0_ref[...]
        extm[:, 0:SUBLANES, :] = tm0_ref[...]
        m_ref[...] = m0_ref[...]

        def load(s, carry):
            cst_ref[s] = jnp.zeros((HP, HW), f32)
            gst_ref[s] = jnp.zeros((GLA_KP, HP), f32)
            for h in range(GLA_H):
                gst_ref[s, h * GLA_DK:(h + 1) * GLA_DK, 0:GLA_DV] = s0_ref[s, h]
            for h in range(ML_H):
                cst_ref[s, 0:ML_D, h * HP:h * HP + ML_D] = c0_ref[s, h]
                n_col = jnp.sum(jnp.where(eye_d, n0_ref[s, h:h + 1, :], 0.0), axis=1, keepdims=True)
                cst_ref[s, 0:ML_D, h * HP + ONE_LANE:h * HP + ONE_LANE + 1] = n_col
            return carry

        lax.fori_loop(0, sb, load, 0)

    x = x_ref[...].reshape(n_rows, D_MODEL)
    hn = _rms(x, vec(V_MIX_G, D_MODEL)).astype(bf16)
    z_ref[...] = _dot(hn, win_ref[...])

    xr = _causal_conv(extr, z_ref[:, Z_UR:Z_UR + LRU_W], vec(V_LRU_CW, LRU_W, CONV_W), vec(V_LRU_CB, LRU_W),
                      sb, t_len)
    xr_b = xr.astype(bf16)
    r = jax.nn.sigmoid(_dot(xr_b, wa_ref[...]) + vec(V_LRU_BA, LRU_W))
    ig = jax.nn.sigmoid(_dot(xr_b, wx_ref[...]) + vec(V_LRU_BX, LRU_W))
    log_a = -LRU_C * r * jax.nn.softplus(-vec(V_LRU_LAM, LRU_W))
    a = jnp.exp(log_a)
    one_m_a2 = 1.0 - a * a
    root = jnp.where(one_m_a2 > 0.0, one_m_a2 * lax.rsqrt(one_m_a2), 0.0)
    bt = root * (ig * xr)
    tl = _row_time((n_rows, LRU_W), t_len)
    s = 1
    while s < t_len:
        keep = tl >= s
        a_prev = jnp.where(keep, pltpu.roll(a, s, 0), 1.0)
        b_prev = jnp.where(keep, pltpu.roll(bt, s, 0), 0.0)
        bt = a * b_prev + bt
        a = a * a_prev
        s *= 2
    hs = bt + a * _rows_from_seq(h_ref[...], t_len)
    h_ref[...] = _last_rows(hs, sb, t_len)
    yr_ref[...] = jax.nn.gelu(z_ref[:, Z_GR:Z_GR + LRU_W]) * hs

    u_m = z_ref[:, Z_UM:Z_UM + HW]
    cm = jax.nn.silu(_causal_conv(extm, u_m, vec(V_ML_CW, HW, CONV_W), vec(V_ML_CB, HW), sb, t_len))
    cm_ref[...] = cm
    cm_b = cm.astype(bf16)
    mq = _dot(cm_b, wq_ref[...])
    mk = _dot(cm_b, wk_ref[...])
    va = _dot(u_m.astype(bf16), wv_ref[...]) + vec(V_ONES, HW)
    gates = _dot(jnp.concatenate([mq.astype(bf16), mk.astype(bf16), va.astype(bf16)], axis=1),
                 wif_ref[...]) + vec(V_ML_BIF, 2 * LANES)
    li = gates[:, 0:LANES]
    lf = jax.nn.log_sigmoid(gates[:, LANES:2 * LANES])
    tg = _row_time((n_rows, LANES), t_len)
    bcum, m_loc = lf, li
    s = 1
    while s < t_len:
        keep = tg >= s
        b_prev = jnp.where(keep, pltpu.roll(bcum, s, 0), 0.0)
        m_prev_seg = jnp.where(keep, pltpu.roll(m_loc, s, 0), -jnp.inf)
        m_loc = jnp.maximum(m_prev_seg + bcum, m_loc)
        bcum = bcum + b_prev
        s *= 2
    m_old = m_ref[:, 0, :]
    m_old_rows = _rows_from_seq(m_old, t_len)
    m_t = jnp.maximum(bcum + m_old_rows, m_loc)
    m_new = _last_rows(m_t, sb, t_len)
    b_last = _last_rows(bcum, sb, t_len)
    m_ref[...] = m_new[:, None, :]
    cc = li - bcum
    cc_ref[...] = cc
    e2 = e2_ref[...]
    rtx_ref[...] = _expand(bcum - m_t, ert2_ref[...])
    wix_ref[...] = jnp.exp(_expand(bcum + m_old_rows - m_t, e2))
    eix_ref[...] = jnp.exp(_expand(-m_t, e2))
    w_k = jnp.exp(_expand(_rows_from_seq(b_last, t_len) + cc - _rows_from_seq(m_new, t_len), e2))
    decx_ref[...] = jnp.exp(_expand(b_last + m_old - m_new, e2))[:, None, :]
    mq_ref[...] = (mq * (ML_D ** -0.5)).astype(mq_ref.dtype)
    mk_ref[...] = mk.astype(mk_ref.dtype)
    kw_ref[...] = (mk * w_k).astype(kw_ref.dtype)
    va_ref[...] = va.astype(va_ref.dtype)

    sub = min(GLA_SUB, t_len)
    n_sub = t_len // sub
    al_b = z_ref[:, Z_AL:Z_AL + LANES].astype(bf16)
    lg = jax.nn.log_sigmoid(_dot(al_b, wup_ref[...]) + vec(V_GLA_BUP, GLA_KP)) * (1.0 / GLA_TAU)
    bcl = _seg_cumsum(lg, sub)
    gq = z_ref[:, Z_QG:Z_QG + GLA_KP] * (GLA_DK ** -0.5)
    gk = z_ref[:, Z_KG:Z_KG + GLA_KP]
    last = _last_rows(bcl, n_rows // sub, sub)
    dec = jnp.exp(last).reshape(sb, n_sub, GLA_KP)
    gdec_ref[...] = jnp.concatenate([dec, jnp.ones((sb, SUBLANES - n_sub, GLA_KP), f32)],
                                    axis=1).reshape(sb * SUBLANES, GLA_KP)
    qt_ref[...] = gq * jnp.exp(bcl)
    kh_ref[...] = gk * jnp.exp(_rows_from_seq(last, sub) - bcl)
    n_grp = n_rows // sub

    def grp_row(arr, i):
        picked = arr.reshape(n_grp, sub, arr.shape[-1])[:, i, :]
        return jnp.broadcast_to(picked[:, None, :], (n_grp, sub, arr.shape[-1])).reshape(arr.shape)

    ts = _row_time((n_rows, GLA_KP), sub)
    key_head_r = _key_head(lax.broadcasted_iota(jnp.int32, (GLA_KP, LANES), 0))
    out_lane = lax.broadcasted_iota(jnp.int32, (GLA_KP, LANES), 1)
    out_src = jnp.where(key_head_r == out_lane // sub, out_lane % sub, -1)

    t4 = ML_H * t_len
    causal_cat = (lax.broadcasted_iota(jnp.int32, (t_len, t4), 0)
                  >= lax.broadcasted_iota(jnp.int32, (t_len, t4), 1) % t_len)
    stack_diag = (lax.broadcasted_iota(jnp.int32, (t4, HW), 0) // t_len
                  == lax.broadcasted_iota(jnp.int32, (t4, HW), 1) // HP)
    state_diag = (lax.broadcasted_iota(jnp.int32, (HW, HW), 0) // HP
                  == lax.broadcasted_iota(jnp.int32, (HW, HW), 1) // HP)
    lane_head = lax.broadcasted_iota(jnp.int32, (t_len, HW), 1) // HP

    def mlstm_seq(s):
        rows = pl.ds(s * t_len, t_len)
        q_b = mq_ref[rows, :].astype(bf16)
        k_bd = jnp.where(stack_diag, jnp.concatenate([mk_ref[rows, :]] * ML_H, axis=0), 0.0).astype(bf16)
        v_bd = jnp.where(stack_diag, jnp.concatenate([va_ref[rows, :]] * ML_H, axis=0), 0.0).astype(bf16)
        c_t = cc_ref[rows, :].T
        c_row = jnp.concatenate([c_t[h:h + 1, :] for h in range(ML_H)], axis=1)
        dmat = jnp.where(causal_cat, rtx_ref[rows, :] + c_row, -jnp.inf)
        sc = _dot_nt(q_b, k_bd) * jnp.exp(dmat)
        c_old = cst_ref[s]
        c_bd = jnp.where(state_diag, jnp.concatenate([c_old.astype(bf16)] * ML_H, axis=0), 0.0)
        num = _dot(sc.astype(bf16), v_bd) + _dot(q_b, c_bd) * wix_ref[rows, :]
        den = jnp.zeros((t_len, HW), f32)
        for h in range(ML_H):
            den = jnp.where(lane_head == h, num[:, h * HP + ONE_LANE:h * HP + ONE_LANE + 1], den)
        hm_ref[rows, :] = num / jnp.maximum(jnp.abs(den), eix_ref[rows, :])
        kw_st = jnp.concatenate([kw_ref[rows, h * HP:(h + 1) * HP] for h in range(ML_H)],
                                axis=0).astype(bf16)
        cst_ref[s] = c_old * decx_ref[s] + _dot_tn(kw_st, v_bd)

    a_cat = jnp.zeros((n_rows, LANES), f32)
    for part in range(INTERLEAVE_PARTS):
        for i in range(part * sub // INTERLEAVE_PARTS, (part + 1) * sub // INTERLEAVE_PARTS):
            e = jnp.where(ts >= i, gq * grp_row(gk, i) * jnp.exp(bcl - grp_row(bcl, i)), 0.0)
            a_cat = a_cat + _dot(e.astype(bf16), jnp.where(out_src == i, 1.0, 0.0).astype(bf16))
        for s in range(part * sb // INTERLEAVE_PARTS, (part + 1) * sb // INTERLEAVE_PARTS):
            mlstm_seq(s)
    ex_r = lax.broadcasted_iota(jnp.int32, (LANES, t4), 0)
    ex_l = lax.broadcasted_iota(jnp.int32, (LANES, t4), 1)
    ex = jnp.where(ex_r // sub == ex_l // t_len, jnp.where(ex_r % sub == ex_l % sub, 1.0, 0.0), 0.0).astype(bf16)
    row_grp = (lax.broadcasted_iota(jnp.int32, (n_rows, t4), 0) % t_len) // sub
    lane_grp = (lax.broadcasted_iota(jnp.int32, (n_rows, t4), 1) % t_len) // sub
    abig_ref[...] = jnp.where(row_grp == lane_grp, _dot(a_cat.astype(bf16), ex), 0.0)

    stack_head = lax.broadcasted_iota(jnp.int32, (GLA_H * sub, GLA_KP), 0) // sub
    key_head = _key_head(lax.broadcasted_iota(jnp.int32, (GLA_H * sub, GLA_KP), 1))
    head_diag = stack_head == key_head

    def gla_seq(s):
        st = gst_ref[s]
        dec_t = gdec_ref[pl.ds(s * SUBLANES, SUBLANES), :].T
        rows_t = pl.ds(s * t_len, t_len)
        gv_bd = jnp.where(stack_diag, jnp.concatenate([z_ref[rows_t, Z_VG:Z_VG + HW]] * GLA_H, axis=0),
                          0.0).astype(bf16)
        og_ref[rows_t, :] = _dot(abig_ref[rows_t, :].astype(bf16), gv_bd)
        for j in range(n_sub):
            rows = pl.ds(s * t_len + j * sub, sub)
            q_bd =jnp.where(head_diag, jnp.concatenate([qt_ref[rows, :]] * GLA_H, axis=0), 0.0).astype(bf16)
            k_bd = jnp.where(head_diag, jnp.concatenate([kh_ref[rows, :]] * GLA_H, axis=0), 0.0).astype(bf16)
            v_st = jnp.concatenate([z_ref[rows, Z_VG + h * HP:Z_VG + (h + 1) * HP] for h in range(GLA_H)],
                                   axis=0).astype(bf16)
            o4 = _dot(q_bd, st.astype(bf16))
            for h in range(GLA_H):
                lanes = slice(h * HP, (h + 1) * HP)
                og_ref[rows, lanes] = og_ref[rows, lanes] + o4[h * sub:(h + 1) * sub, :]
            st = st * dec_t[:, j:j + 1] + _dot_tn(k_bd, v_st)
        gst_ref[s] = st

    for s in range(sb):
        gla_seq(s)

    def head_norm(o):
        real = lax.broadcasted_iota(jnp.int32, (n_rows, HP), 1) < ML_D
        normed = []
        for h in range(ML_H):
            seg = jnp.where(real, o[:, h * HP:(h + 1) * HP], 0.0)
            ms = jnp.sum(seg * seg, axis=-1, keepdims=True) * (1.0 / ML_D)
            normed.append(seg * lax.rsqrt(ms + EPS))
        return jnp.concatenate(normed, axis=1)

    y_m = (jax.nn.sigmoid(z_ref[:, Z_ZM:Z_ZM + HW])
           * (head_norm(hm_ref[...]) * vec(V_ML_G, HW) + vec(V_ML_SKIP, HW) * cm_ref[...]))
    y_g = head_norm(og_ref[...]) * vec(V_GLA_G, HW) * jax.nn.silu(z_ref[:, Z_GG:Z_GG + HW])
    y = (_dot(yr_ref[...].astype(bf16), wout_ref[0:LRU_W, :])
         + _dot(y_m.astype(bf16), wout_ref[LRU_W:LRU_W + HW, :])
         + _dot(y_g.astype(bf16), wout_ref[LRU_W + HW:, :]))
    y_ref[...] = x_ref[...] + y.reshape(sb, t_len, D_MODEL)

    @pl.when(chunk == pl.num_programs(1) - 1)
    def _store_state():
        tro_ref[...] = extr[:, 0:SUBLANES, :]
        tmo_ref[...] = extm[:, 0:SUBLANES, :]

        def store(s, carry):
            for h in range(GLA_H):
                s_ref[s, h] = gst_ref[s, h * GLA_DK:(h + 1) * GLA_DK, 0:GLA_DV]
            for h in range(ML_H):
                c_ref[s, h] = cst_ref[s, 0:ML_D, h * HP:h * HP + ML_D]
                n_col = cst_ref[s, 0:ML_D, h * HP + ONE_LANE:h * HP + ONE_LANE + 1]
                n_ref[s, h:h + 1, :] = jnp.sum(jnp.where(eye_d, n_col, 0.0), axis=0, keepdims=True)
            return carry

        lax.fori_loop(0, sb, store, 0)


def _mix(x3d, states, lw, consts, *, layer, sb, t_len):
    n_seq, length, _ = x3d.shape
    n_rows = sb * t_len
    n_sub = t_len // min(GLA_SUB, t_len)
    grid = (n_seq // sb, length // t_len)

    def seq_spec(arr):
        blk = (None, sb) + arr.shape[2:]
        zeros = (0,) * (arr.ndim - 2)
        return pl.BlockSpec(blk, lambda i, c: (layer, i) + zeros, pipeline_mode=pl.Buffered(1))

    weights = [lw[k] for k in _MIX_WEIGHT_KEYS]
    assert len(states) == N_STATES and len(consts) == N_MIX_CONSTS
    x_spec = pl.BlockSpec((sb, t_len, D_MODEL), lambda i, c: (i, c, 0))
    row_scratch = lambda width, dtype=f32: pltpu.VMEM((n_rows, width), dtype)
    opnd = bf16 if t_len % BF16_SUBLANES == 0 else f32
    outs = pl.pallas_call(
        functools.partial(_mix_body, sb=sb, t_len=t_len),
        out_shape=[jax.ShapeDtypeStruct(x3d.shape, f32)] + [jax.ShapeDtypeStruct(a.shape, f32) for a in states],
        grid=grid,
        in_specs=([x_spec] + [seq_spec(a) for a in states] + [_layer_spec(w.shape, layer) for w in weights]
                  + [_const_spec(c.shape) for c in consts]),
        out_specs=[x_spec] + [seq_spec(a) for a in states],
        input_output_aliases={1 + k: 1 + k for k in range(N_STATES)},
        scratch_shapes=[
            pltpu.VMEM((sb, t_len + SUBLANES, LRU_W), f32),
            pltpu.VMEM((sb, t_len + SUBLANES, HW), f32),
            row_scratch(Z_W),
            row_scratch(HW, opnd),
            row_scratch(HW, opnd),
            row_scratch(HW, opnd),
            row_scratch(HW, opnd),
            row_scratch(ML_H * t_len),
            row_scratch(HW),
            row_scratch(HW),
            row_scratch(LANES),
            pltpu.VMEM((sb, 1, HW), f32),
            row_scratch(HW),
            row_scratch(GLA_KP),
            row_scratch(GLA_KP),
            pltpu.VMEM((sb * SUBLANES, GLA_KP), f32),
            row_scratch(ML_H * t_len),
            row_scratch(HW),
            row_scratch(LRU_W),
            row_scratch(HW),
            pltpu.VMEM((sb, HP, HW), f32),
            pltpu.VMEM((sb, GLA_KP, HP), f32),
        ],
        compiler_params=pltpu.CompilerParams(
            dimension_semantics=("arbitrary", "arbitrary"), vmem_limit_bytes=VMEM_LIMIT_BYTES),
        name="mixer",
    )(x3d, *states, *weights, *consts)
    return outs[0], list(outs[1:])


def _pad_last(w, width):
    return jnp.pad(w, [(0, 0)] * (w.ndim - 1) + [(0, width - w.shape[-1])])


def _pad_heads(w, d, dp):
    lead = w.shape[:-1]
    w = _pad_last(w.reshape(lead + (w.shape[-1] // d, d)), dp)
    return w.reshape(lead + (-1,))


def _pad_head_rows(w, d, dp):
    return jnp.swapaxes(_pad_heads(jnp.swapaxes(w, -1, -2), d, dp), -1, -2)


def _block_diag(blocks, dp=None):
    depth, n, d, _ = blocks.shape
    dp = d if dp is None else dp
    blocks = jnp.pad(blocks, ((0, 0), (0, 0), (0, dp - d), (0, dp - d)))
    eye = jnp.eye(n, dtype=blocks.dtype)
    return jnp.einsum("lhij,hg->lhigj", blocks, eye).reshape(depth, n * dp, n * dp)


def _vec_rows(v):
    v = v.astype(f32)
    v = v[:, None, :] if v.ndim == 2 else v
    return _pad_last(v, D_MODEL)


def _tail(conv_state, d=None, dp=None):
    conv_state = conv_state.astype(f32)
    if d is not None:
        conv_state = _pad_heads(conv_state, d, dp)
    return jnp.pad(conv_state, ((0, 0), (0, 0), (SUBLANES - (CONV_W - 1), 0), (0, 0)))


def _untail(tail, d=None, dp=None):
    tail = tail[:, :, SUBLANES - (CONV_W - 1):, :]
    if d is not None:
        tail = tail.reshape(tail.shape[:3] + (-1, dp))[..., :d].reshape(tail.shape[:3] + (-1,))
    return tail


def _mix_consts(t_len):
    head = jnp.arange(LANES)[:, None]
    lane = jnp.arange(HW)
    e = (head == (lane // HP)[None, :]).astype(bf16)
    e_rt = (head == (jnp.arange(ML_H * t_len) // t_len)[None, :]).astype(bf16)
    return jnp.concatenate([e, e], axis=0), jnp.concatenate([e_rt, e_rt], axis=0)


def kernel(x_prompt, x_sample, state_lru_h, state_lru_conv, state_mlstm_C, state_mlstm_n, state_mlstm_m, state_mlstm_conv, state_gla_S, meta_tokens, ffn1_norm_g, ffn1_w1, ffn1_w3, ffn1_w2, mix_norm_g, w_in, lru_conv_w, lru_conv_b, lru_wa, lru_ba, lru_wx, lru_bx, lru_lambda, ml_conv_w, ml_conv_b, ml_wq, ml_wk, ml_wv, ml_w_if, ml_b_if, ml_norm_g, ml_skip, gla_w_up, gla_b_up, gla_norm_g, w_out, ffn2_norm_g, ffn2_w1, ffn2_w3, ffn2_w2, final_norm_g):
    n_prompt = x_prompt.shape[0]
    n_sample, t_sample, _ = x_sample.shape

    sizes = (LRU_W, LRU_W, ML_W, ML_W, GLA_KW, GLA_KW, GLA_VW, GLA_VW, GLA_RANK)
    widths = (LRU_W, LRU_W, HW, HW, GLA_KP, GLA_KP, HW, HW, LANES)
    head_pads = (None, None, (ML_D, HP), (ML_D, HP), None, None, (GLA_DV, HP), (GLA_DV, HP), None)
    parts, off = [], 0
    for size, width, pad in zip(sizes, widths, head_pads):
        part = w_in[..., off:off + size]
        parts.append(_pad_last(part, width) if pad is None else _pad_heads(part, *pad))
        off += size
    wif = jnp.concatenate([_pad_head_rows(ml_w_if[:, i * ML_W:(i + 1) * ML_W], ML_D, HP) for i in range(3)],
                          axis=1)
    wif = jnp.concatenate([_pad_last(wif[..., :ML_H], LANES), _pad_last(wif[..., ML_H:], LANES)], axis=-1)
    bif = jnp.concatenate([_pad_last(ml_b_if[:, :ML_H], LANES), _pad_last(ml_b_if[:, ML_H:], LANES)], axis=-1)
    wo = jnp.concatenate([w_out[:, :LRU_W], _pad_head_rows(w_out[:, LRU_W:LRU_W + ML_W], ML_D, HP),
                          _pad_head_rows(w_out[:, LRU_W + ML_W:], GLA_DV, HP)], axis=1)
    ones_row = jnp.broadcast_to(((jnp.arange(HW) % HP) == ONE_LANE).astype(f32), (DEPTH, HW))
    vec_rows = [mix_norm_g, lru_conv_w, lru_conv_b, lru_ba, lru_bx, lru_lambda,
                _pad_heads(ml_conv_w, ML_D, HP), _pad_heads(ml_conv_b, ML_D, HP), bif,
                _pad_heads(ml_norm_g, ML_D, HP), _pad_heads(ml_skip, ML_D, HP),
                gla_b_up, _pad_heads(gla_norm_g, GLA_DV, HP), ones_row]
    vecs = jnp.concatenate([_vec_rows(v) for v in vec_rows], axis=1)
    vecs = jnp.pad(vecs, ((0, 0), (0, VEC_ROWS - vecs.shape[1]), (0, 0)))
    lw = dict(
        vecs=vecs,
        w_in=jnp.concatenate(parts, axis=-1).astype(bf16),
        lru_wa=_block_diag(lru_wa).astype(bf16), lru_wx=_block_diag(lru_wx).astype(bf16),
        ml_wq=_block_diag(ml_wq, HP).astype(bf16), ml_wk=_block_diag(ml_wk, HP).astype(bf16),
        ml_wv=_block_diag(ml_wv, HP).astype(bf16), ml_wif=wif.astype(bf16),
        gla_wup=jnp.pad(gla_w_up, ((0, 0), (0, LANES - GLA_RANK), (0, GLA_KP - GLA_KW))).astype(bf16),
        w_out=wo.astype(bf16))
    ffn1 = (ffn1_norm_g.astype(f32)[:, None, :], ffn1_w1.astype(bf16), ffn1_w3.astype(bf16), ffn1_w2.astype(bf16))
    ffn2 = (ffn2_norm_g.astype(f32)[:, None, :], ffn2_w1.astype(bf16), ffn2_w3.astype(bf16), ffn2_w2.astype(bf16))
    final_g = final_norm_g.astype(f32)[None, :]

    def run_trunk(x3d, states, *, sb, t_len, tm, need_output):
        n_seq, length, _ = x3d.shape
        consts = _mix_consts(t_len)
        for l in range(DEPTH):
            x2d = _ffn(x3d.reshape(n_seq * length, D_MODEL), *ffn1, final_g, layer=l, tm=tm, post_norm=False)
            x3d, states = _mix(x2d.reshape(n_seq, length, D_MODEL), states, lw, consts,
                               layer=l, sb=sb, t_len=t_len)
            last = l == DEPTH - 1
            if need_output or not last:
                x2d = _ffn(x3d.reshape(n_seq * length, D_MODEL), *ffn2, final_g, layer=l, tm=tm, post_norm=last)
                x3d = x2d.reshape(n_seq, length, D_MODEL)
        return x3d, states

    zeros = lambda *shape: jnp.zeros((DEPTH, n_prompt) + shape, f32)
    zero_states = [zeros(LRU_W), zeros(SUBLANES, LRU_W), zeros(SUBLANES, HW), zeros(ML_H, ML_D, ML_D),
                   zeros(ML_H, ML_D), zeros(1, LANES), zeros(GLA_H, GLA_DK, GLA_DV)]
    x_meta = jnp.broadcast_to(meta_tokens.astype(f32)[None], (n_prompt, N_META, D_MODEL))
    _, meta_states = run_trunk(x_meta, zero_states, sb=n_prompt, t_len=N_META, tm=n_prompt * N_META,
                               need_output=False)

    y_prompt, p_states = run_trunk(x_prompt, meta_states, sb=n_prompt, t_len=CHUNK, tm=FFN_ROWS, need_output=True)

    s_states = [state_lru_h.astype(f32), _tail(state_lru_conv), _tail(state_mlstm_conv, ML_D, HP),
                state_mlstm_C.astype(f32), state_mlstm_n.astype(f32),
                _pad_last(state_mlstm_m.astype(f32), LANES)[:, :, None, :], state_gla_S.astype(f32)]
    y_sample, s_states = run_trunk(x_sample, s_states, sb=SAMPLE_SEQS, t_len=t_sample, tm=FFN_ROWS,
                                   need_output=True)

    def reference_layout(st):
        lru_h, lru_tail, ml_tail, ml_c, ml_n, ml_m, gla_s = st
        return (lru_h, _untail(lru_tail), ml_c, ml_n, ml_m[:, :, 0, :ML_H], _untail(ml_tail, ML_D, HP),
                gla_s)

    return (y_prompt, y_sample) + reference_layout(p_states) + reference_layout(s_states)
```

```python
import functools

import jax
import jax.numpy as jnp
from jax import lax
from jax.experimental import pallas as pl
from jax.experimental.pallas import tpu as pltpu

f32 = jnp.float32
bf16 = jnp.bfloat16

D_MODEL = 1024
DEPTH = 2
N_META = 16
CONV_W = 4
CHUNK = 64
LRU_W = 256
LRU_BLOCKS = 4
LRU_C = 8.0
ML_H = 4
ML_D = 96
ML_W = ML_H * ML_D
GLA_H = 4
GLA_DK = 48
GLA_DV = 96
GLA_KW = GLA_H * GLA_DK
GLA_VW = GLA_H * GLA_DV
GLA_RANK = 16
GLA_TAU = 16.0
D_FF = 2816
EPS = 1e-6

SUBLANES = 8
LANES = 128
BF16_SUBLANES = 16
VMEM_LIMIT_BYTES = 60 * 1024 * 1024

FFN_ROWS = 512
SAMPLE_SEQS = 16
FF_TILE = 256
N_FF_TILES = D_FF // FF_TILE
GLA_SUB = 16
INTERLEAVE_PARTS = 4

HP = LANES
HW = ML_H * HP
ONE_LANE = ML_D
GLA_KP = 2 * LANES

Z_UR = 0
Z_GR = Z_UR + LRU_W
Z_UM = Z_GR + LRU_W
Z_ZM = Z_UM + HW
Z_QG = Z_ZM + HW
Z_KG = Z_QG + GLA_KP
Z_VG = Z_KG + GLA_KP
Z_GG = Z_VG + HW
Z_AL = Z_GG + HW
Z_W = Z_AL + LANES


def _dot(a, b):
    return jnp.dot(a, b, preferred_element_type=f32)


def _dot_nt(a, b):
    return lax.dot_general(a, b, (((1,), (1,)), ((), ())), preferred_element_type=f32)


def _dot_tn(a, b):
    return lax.dot_general(a, b, (((0,), (0,)), ((), ())), preferred_element_type=f32)


def _rms(x, g):
    return x * lax.rsqrt(jnp.mean(x * x, axis=-1, keepdims=True) + EPS) * g


FFN_STAGE_REFS = 4


def _ffn_body(*refs, n_stages, post_norm):
    x_ref = refs[0]
    gf_ref, o_ref, gated_ref = refs[1 + FFN_STAGE_REFS * n_stages:]
    y = x_ref[...]
    for k in range(n_stages):
        g_ref, w1_ref, w3_ref, w2_ref = refs[1 + FFN_STAGE_REFS * k:1 + FFN_STAGE_REFS * (k + 1)]
        h = _rms(y, g_ref[...]).astype(bf16)
        for f in range(N_FF_TILES):
            cols = slice(f * FF_TILE, (f + 1) * FF_TILE)
            a = _dot(h, w1_ref[:, cols])
            b = _dot(h, w3_ref[:, cols])
            gated_ref[:, cols] = (a * jax.nn.sigmoid(a) * b).astype(bf16)
        y = y + 0.5 * _dot(gated_ref[...], w2_ref[...])
    if post_norm:
        y = _rms(y, gf_ref[...])
    o_ref[...] = y


def _const_spec(shape):
    zeros = (0,) * len(shape)
    return pl.BlockSpec(shape, lambda *_: zeros, pipeline_mode=pl.Buffered(1))


def _layer_spec(shape, layer):
    zeros = (0,) * (len(shape) - 1)
    return pl.BlockSpec((None,) + tuple(shape[1:]), lambda *_: (layer,) + zeros, pipeline_mode=pl.Buffered(1))


def _ffn(x2d, stages, gf, *, tm, post_norm):
    rows = x2d.shape[0]
    weights = [w for ws, _ in stages for w in ws]
    specs = [_layer_spec(w.shape, layer) for ws, layer in stages for w in ws]
    return pl.pallas_call(
        functools.partial(_ffn_body, n_stages=len(stages), post_norm=post_norm),
        out_shape=jax.ShapeDtypeStruct(x2d.shape, f32),
        grid=(rows // tm,),
        in_specs=[pl.BlockSpec((tm, D_MODEL), lambda i: (i, 0))] + specs + [_const_spec(gf.shape)],
        out_specs=pl.BlockSpec((tm, D_MODEL), lambda i: (i, 0)),
        scratch_shapes=[pltpu.VMEM((tm, D_FF), bf16)],
        compiler_params=pltpu.CompilerParams(
            dimension_semantics=("arbitrary",), vmem_limit_bytes=VMEM_LIMIT_BYTES),
        name="ffn",
    )(x2d, *weights, gf)


def _row_time(shape, period):
    return lax.broadcasted_iota(jnp.int32, shape, 0) % period


def _seg_cumsum(x, period):
    t = _row_time(x.shape, period)
    s = 1
    while s < period:
        x = x + jnp.where(t >= s, pltpu.roll(x, s, 0), 0.0)
        s *= 2
    return x


def _rows_from_seq(v, t_len):
    sb, width = v.shape
    return jnp.broadcast_to(v[:, None, :], (sb, t_len, width)).reshape(sb * t_len, width)


def _last_rows(x, groups, period):
    return x.reshape(groups, period, x.shape[-1])[:, period - 1, :]


def _key_head(idx):
    head = jnp.zeros_like(idx)
    for h in range(1, GLA_H + 1):
        head = head + (idx >= h * GLA_DK).astype(jnp.int32)
    return head


def _expand(x, e2):
    hi = x.astype(bf16)
    lo = (x - hi.astype(f32)).astype(bf16)
    return _dot(jnp.concatenate([hi, lo], axis=1), e2)


def _causal_conv(ext_ref, u, w, b, sb, t_len):
    width = u.shape[-1]
    ext_ref[:, SUBLANES:SUBLANES + t_len, :] = u.reshape(sb, t_len, width)
    base = SUBLANES - (CONV_W - 1)
    out = b + ext_ref[:, base:base + t_len, :] * w[0:1, :]
    for j in range(1, CONV_W):
        out = out + ext_ref[:, base + j:base + j + t_len, :] * w[j:j + 1, :]
    ext_ref[:, 0:SUBLANES, :] = ext_ref[:, t_len:t_len + SUBLANES, :]
    return out.reshape(sb * t_len, width)


V_MIX_G = 0
V_LRU_CW = 1
V_LRU_CB = 5
V_LRU_BA = 6
V_LRU_BX = 7
V_LRU_LAM = 8
V_ML_CW = 9
V_ML_CB = 13
V_ML_BIF = 14
V_ML_G = 15
V_ML_SKIP = 16
V_GLA_BUP = 17
V_GLA_G = 18
V_ONES = 19
VEC_ROWS = 24
N_STATES = 7
_MIX_WEIGHT_KEYS = ("vecs", "w_in", "lru_wa", "lru_wx", "ml_wq", "ml_wk", "ml_wv", "ml_wif", "gla_wup", "w_out")
N_MIX_CONSTS = 2
N_MIX_INPUTS = 1 + N_STATES + len(_MIX_WEIGHT_KEYS) + N_MIX_CONSTS


def _mix_body(*refs, sb, t_len):
    (x_ref, h0_ref, tr0_ref, tm0_ref, c0_ref, n0_ref, m0_ref, s0_ref,
     vec_ref, win_ref, wa_ref, wx_ref, wq_ref, wk_ref, wv_ref, wif_ref, wup_ref, wout_ref,
     e2_ref, ert2_ref) = refs[:N_MIX_INPUTS]
    (y_ref, h_ref, tro_ref, tmo_ref, c_ref, n_ref, m_ref, s_ref,
     extr, extm, z_ref, mq_ref, mk_ref, kw_ref, va_ref, rtx_ref, wix_ref, eix_ref, cc_ref,
     decx_ref, hm_ref, qt_ref, kh_ref, gdec_ref, abig_ref, og_ref, yr_ref, cm_ref,
     cst_ref, gst_ref) = refs[N_MIX_INPUTS:]

    def vec(row, width, n=1):
        return vec_ref[row:row + n, 0:width]

    chunk = pl.program_id(1)
    n_rows = sb * t_len
    eye_d = (lax.broadcasted_iota(jnp.int32, (ML_D, ML_D), 0)
             == lax.broadcasted_iota(jnp.int32, (ML_D, ML_D), 1))

    @pl.when(chunk == 0)
    def _load_state():
        h_ref[...] = h0_ref[...]
        extr[:, 0:SUBLANES, :] = tr0_ref[...]
        extm[:, 0:SUBLANES, :] = tm0_ref[...]
        m_ref[...] = m0_ref[...]

        def load(s, carry):
            cst_ref[s] = jnp.zeros((HP, HW), f32)
            gst_ref[s] = jnp.zeros((GLA_KP, HP), f32)
            for h in range(GLA_H):
                gst_ref[s, h * GLA_DK:(h + 1) * GLA_DK, 0:GLA_DV] = s0_ref[s, h]
            for h in range(ML_H):
                cst_ref[s, 0:ML_D, h * HP:h * HP + ML_D] = c0_ref[s, h]
                n_col = jnp.sum(jnp.where(eye_d, n0_ref[s, h:h + 1, :], 0.0), axis=1, keepdims=True)
                cst_ref[s, 0:ML_D, h * HP + ONE_LANE:h * HP + ONE_LANE + 1] = n_col
            return carry

        lax.fori_loop(0, sb, load, 0)

    x = x_ref[...].reshape(n_rows, D_MODEL)
    hn = _rms(x, vec(V_MIX_G, D_MODEL)).astype(bf16)
    z_ref[...] = _dot(hn, win_ref[...])

    xr = _causal_conv(extr, z_ref[:, Z_UR:Z_UR + LRU_W], vec(V_LRU_CW, LRU_W, CONV_W), vec(V_LRU_CB, LRU_W),
                      sb, t_len)
    xr_b = xr.astype(bf16)
    r = jax.nn.sigmoid(_dot(xr_b, wa_ref[...]) + vec(V_LRU_BA, LRU_W))
    ig = jax.nn.sigmoid(_dot(xr_b, wx_ref[...]) + vec(V_LRU_BX, LRU_W))
    log_a = -LRU_C * r * jax.nn.softplus(-vec(V_LRU_LAM, LRU_W))
    a = jnp.exp(log_a)
    one_m_a2 = 1.0 - a * a
    root = jnp.where(one_m_a2 > 0.0, one_m_a2 * lax.rsqrt(one_m_a2), 0.0)
    bt = root * (ig * xr)
    tl = _row_time((n_rows, LRU_W), t_len)
    s = 1
    while s < t_len:
        keep = tl >= s
        a_prev = jnp.where(keep, pltpu.roll(a, s, 0), 1.0)
        b_prev = jnp.where(keep, pltpu.roll(bt, s, 0), 0.0)
        bt = a * b_prev + bt
        a = a * a_prev
        s *= 2
    hs = bt + a * _rows_from_seq(h_ref[...], t_len)
    h_ref[...] = _last_rows(hs, sb, t_len)
    yr_ref[...] = jax.nn.gelu(z_ref[:, Z_GR:Z_GR + LRU_W]) * hs

    u_m = z_ref[:, Z_UM:Z_UM + HW]
    cm = jax.nn.silu(_causal_conv(extm, u_m, vec(V_ML_CW, HW, CONV_W), vec(V_ML_CB, HW), sb, t_len))
    cm_ref[...] = cm
    cm_b = cm.astype(bf16)
    mq = _dot(cm_b, wq_ref[...])
    mk = _dot(cm_b, wk_ref[...])
    va = _dot(u_m.astype(bf16), wv_ref[...]) + vec(V_ONES, HW)
    gates = _dot(jnp.concatenate([mq.astype(bf16), mk.astype(bf16), va.astype(bf16)], axis=1),
                 wif_ref[...]) + vec(V_ML_BIF, 2 * LANES)
    li = gates[:, 0:LANES]
    lf = jax.nn.log_sigmoid(gates[:, LANES:2 * LANES])
    tg = _row_time((n_rows, LANES), t_len)
    bcum, m_loc = lf, li
    s = 1
    while s < t_len:
        keep = tg >= s
        b_prev = jnp.where(keep, pltpu.roll(bcum, s, 0), 0.0)
        m_prev_seg = jnp.where(keep, pltpu.roll(m_loc, s, 0), -jnp.inf)
        m_loc = jnp.maximum(m_prev_seg + bcum, m_loc)
        bcum = bcum + b_prev
        s *= 2
    m_old = m_ref[:, 0, :]
    m_old_rows = _rows_from_seq(m_old, t_len)
    m_t = jnp.maximum(bcum + m_old_rows, m_loc)
    m_new = _last_rows(m_t, sb, t_len)
    b_last = _last_rows(bcum, sb, t_len)
    m_ref[...] = m_new[:, None, :]
    cc = li - bcum
    cc_ref[...] = cc
    e2 = e2_ref[...]
    rtx_ref[...] = _expand(bcum - m_t, ert2_ref[...])
    wix_ref[...] = jnp.exp(_expand(bcum + m_old_rows - m_t, e2))
    eix_ref[...] = jnp.exp(_expand(-m_t, e2))
    w_k = jnp.exp(_expand(_rows_from_seq(b_last, t_len) + cc - _rows_from_seq(m_new, t_len), e2))
    decx_ref[...] = jnp.exp(_expand(b_last + m_old - m_new, e2))[:, None, :]
    mq_ref[...] = (mq * (ML_D ** -0.5)).astype(mq_ref.dtype)
    mk_ref[...] = mk.astype(mk_ref.dtype)
    kw_ref[...] = (mk * w_k).astype(kw_ref.dtype)
    va_ref[...] = va.astype(va_ref.dtype)

    sub = min(GLA_SUB, t_len)
    n_sub = t_len // sub
    al_b = z_ref[:, Z_AL:Z_AL + LANES].astype(bf16)
    lg = jax.nn.log_sigmoid(_dot(al_b, wup_ref[...]) + vec(V_GLA_BUP, GLA_KP)) * (1.0 / GLA_TAU)
    bcl = _seg_cumsum(lg, sub)
    gq = z_ref[:, Z_QG:Z_QG + GLA_KP] * (GLA_DK ** -0.5)
    gk = z_ref[:, Z_KG:Z_KG + GLA_KP]
    last = _last_rows(bcl, n_rows // sub, sub)
    dec = jnp.exp(last).reshape(sb, n_sub, GLA_KP)
    gdec_ref[...] = jnp.concatenate([dec, jnp.ones((sb, SUBLANES - n_sub, GLA_KP), f32)],
                                    axis=1).reshape(sb * SUBLANES, GLA_KP)
    qt_ref[...] = gq * jnp.exp(bcl)
    kh_ref[...] = gk * jnp.exp(_rows_from_seq(last, sub) - bcl)
    n_grp = n_rows // sub

    def grp_row(arr, i):
        picked = arr.reshape(n_grp, sub, arr.shape[-1])[:, i, :]
        return jnp.broadcast_to(picked[:, None, :], (n_grp, sub, arr.shape[-1])).reshape(arr.shape)

    ts = _row_time((n_rows, GLA_KP), sub)
    key_head_r = _key_head(lax.broadcasted_iota(jnp.int32, (GLA_KP, LANES), 0))
    out_lane = lax.broadcasted_iota(jnp.int32, (GLA_KP, LANES), 1)
    out_src = jnp.where(key_head_r == out_lane // sub, out_lane % sub, -1)

    t4 = ML_H * t_len
    causal_cat = (lax.broadcasted_iota(jnp.int32, (t_len, t4), 0)
                  >= lax.broadcasted_iota(jnp.int32, (t_len, t4), 1) % t_len)
    stack_diag = (lax.broadcasted_iota(jnp.int32, (t4, HW), 0) // t_len
                  == lax.broadcasted_iota(jnp.int32, (t4, HW), 1) // HP)
    state_diag = (lax.broadcasted_iota(jnp.int32, (HW, HW), 0) // HP
                  == lax.broadcasted_iota(jnp.int32, (HW, HW), 1) // HP)
    lane_head = lax.broadcasted_iota(jnp.int32, (t_len, HW), 1) // HP

    def mlstm_seq(s):
        rows = pl.ds(s * t_len, t_len)
        q_b = mq_ref[rows, :].astype(bf16)
        k_bd = jnp.where(stack_diag, jnp.concatenate([mk_ref[rows, :]] * ML_H, axis=0), 0.0).astype(bf16)
        v_bd = jnp.where(stack_diag, jnp.concatenate([va_ref[rows, :]] * ML_H, axis=0), 0.0).astype(bf16)
        c_t = cc_ref[rows, :].T
        c_row = jnp.concatenate([c_t[h:h + 1, :] for h in range(ML_H)], axis=1)
        dmat = jnp.where(causal_cat, rtx_ref[rows, :] + c_row, -jnp.inf)
        sc = _dot_nt(q_b, k_bd) * jnp.exp(dmat)
        c_old = cst_ref[s]
        c_bd = jnp.where(state_diag, jnp.concatenate([c_old.astype(bf16)] * ML_H, axis=0), 0.0)
        num = _dot(sc.astype(bf16), v_bd) + _dot(q_b, c_bd) * wix_ref[rows, :]
        den = jnp.zeros((t_len, HW), f32)
        for h in range(ML_H):
            den = jnp.where(lane_head == h, num[:, h * HP + ONE_LANE:h * HP + ONE_LANE + 1], den)
        hm_ref[rows, :] = num / jnp.maximum(jnp.abs(den), eix_ref[rows, :])
        kw_st = jnp.concatenate([kw_ref[rows, h * HP:(h + 1) * HP] for h in range(ML_H)],
                                axis=0).astype(bf16)
        cst_ref[s] = c_old * decx_ref[s] + _dot_tn(kw_st, v_bd)

    a_cat = jnp.zeros((n_rows, LANES), f32)
    for part in range(INTERLEAVE_PARTS):
        for i in range(part * sub // INTERLEAVE_PARTS, (part + 1) * sub // INTERLEAVE_PARTS):
            e = jnp.where(ts >= i, gq * grp_row(gk, i) * jnp.exp(bcl - grp_row(bcl, i)), 0.0)
            a_cat = a_cat + _dot(e.astype(bf16), jnp.where(out_src == i, 1.0, 0.0).astype(bf16))
        for s in range(part * sb // INTERLEAVE_PARTS, (part + 1) * sb // INTERLEAVE_PARTS):
            mlstm_seq(s)
    ex_r = lax.broadcasted_iota(jnp.int32, (LANES, t4), 0)
    ex_l = lax.broadcasted_iota(jnp.int32, (LANES, t4), 1)
    ex = jnp.where(ex_r // sub == ex_l // t_len, jnp.where(ex_r % sub == ex_l % sub, 1.0, 0.0), 0.0).astype(bf16)
    row_grp = (lax.broadcasted_iota(jnp.int32, (n_rows, t4), 0) % t_len) // sub
    lane_grp = (lax.broadcasted_iota(jnp.int32, (n_rows, t4), 1) % t_len) // sub
    abig_ref[...] = jnp.where(row_grp == lane_grp, _dot(a_cat.astype(bf16), ex), 0.0)

    stack_head = lax.broadcasted_iota(jnp.int32, (GLA_H * sub, GLA_KP), 0) // sub
    key_head = _key_head(lax.broadcasted_iota(jnp.int32, (GLA_H * sub, GLA_KP), 1))
    head_diag = stack_head == key_head

    def gla_seq(s):
        st = gst_ref[s]
        dec_t = gdec_ref[pl.ds(s * SUBLANES, SUBLANES), :].T
        rows_t = pl.ds(s * t_len, t_len)
        gv_bd = jnp.where(stack_diag, jnp.concatenate([z_ref[rows_t, Z_VG:Z_VG + HW]] * GLA_H, axis=0),
                          0.0).astype(bf16)
        og_ref[rows_t, :] = _dot(abig_ref[rows_t, :].astype(bf16), gv_bd)
        for j in range(n_sub):
            rows = pl.ds(s * t_len + j * sub, sub)
            q_bd =jnp.where(head_diag, jnp.concatenate([qt_ref[rows, :]] * GLA_H, axis=0), 0.0).astype(bf16)
            k_bd = jnp.where(head_diag, jnp.concatenate([kh_ref[rows, :]] * GLA_H, axis=0), 0.0).astype(bf16)
            v_st = jnp.concatenate([z_ref[rows, Z_VG + h * HP:Z_VG + (h + 1) * HP] for h in range(GLA_H)],
                                   axis=0).astype(bf16)
            o4 = _dot(q_bd, st.astype(bf16))
            for h in range(GLA_H):
                lanes = slice(h * HP, (h + 1) * HP)
                og_ref[rows, lanes] = og_ref[rows, lanes] + o4[h * sub:(h + 1) * sub, :]
            st = st * dec_t[:, j:j + 1] + _dot_tn(k_bd, v_st)
        gst_ref[s] = st

    for s in range(sb):
        gla_seq(s)

    def head_norm(o):
        real = lax.broadcasted_iota(jnp.int32, (n_rows, HP), 1) < ML_D
        normed = []
        for h in range(ML_H):
            seg = jnp.where(real, o[:, h * HP:(h + 1) * HP], 0.0)
            ms = jnp.sum(seg * seg, axis=-1, keepdims=True) * (1.0 / ML_D)
            normed.append(seg * lax.rsqrt(ms + EPS))
        return jnp.concatenate(normed, axis=1)

    y_m = (jax.nn.sigmoid(z_ref[:, Z_ZM:Z_ZM + HW])
           * (head_norm(hm_ref[...]) * vec(V_ML_G, HW) + vec(V_ML_SKIP, HW) * cm_ref[...]))
    y_g = head_norm(og_ref[...]) * vec(V_GLA_G, HW) * jax.nn.silu(z_ref[:, Z_GG:Z_GG + HW])
    y = (_dot(yr_ref[...].astype(bf16), wout_ref[0:LRU_W, :])
         + _dot(y_m.astype(bf16), wout_ref[LRU_W:LRU_W + HW, :])
         + _dot(y_g.astype(bf16), wout_ref[LRU_W + HW:, :]))
    y_ref[...] = x_ref[...] + y.reshape(sb, t_len, D_MODEL)

    @pl.when(chunk == pl.num_programs(1) - 1)
    def _store_state():
        tro_ref[...] = extr[:, 0:SUBLANES, :]
        tmo_ref[...] = extm[:, 0:SUBLANES, :]

        def store(s, carry):
            for h in range(GLA_H):
                s_ref[s, h] = gst_ref[s, h * GLA_DK:(h + 1) * GLA_DK, 0:GLA_DV]
            for h in range(ML_H):
                c_ref[s, h] = cst_ref[s, 0:ML_D, h * HP:h * HP + ML_D]
                n_col = cst_ref[s, 0:ML_D, h * HP + ONE_LANE:h * HP + ONE_LANE + 1]
                n_ref[s, h:h + 1, :] = jnp.sum(jnp.where(eye_d, n_col, 0.0), axis=0, keepdims=True)
            return carry

        lax.fori_loop(0, sb, store, 0)


def _mix(x3d, states, lw, consts, *, layer, sb, t_len):
    n_seq, length, _ = x3d.shape
    n_rows = sb * t_len
    n_sub = t_len // min(GLA_SUB, t_len)
    grid = (n_seq // sb, length // t_len)

    def seq_spec(arr):
        blk = (None, sb) + arr.shape[2:]
        zeros = (0,) * (arr.ndim - 2)
        return pl.BlockSpec(blk, lambda i, c: (layer, i) + zeros, pipeline_mode=pl.Buffered(1))

    weights = [lw[k] for k in _MIX_WEIGHT_KEYS]
    assert len(states) == N_STATES and len(consts) == N_MIX_CONSTS
    x_spec = pl.BlockSpec((sb, t_len, D_MODEL), lambda i, c: (i, c, 0))
    row_scratch = lambda width, dtype=f32: pltpu.VMEM((n_rows, width), dtype)
    opnd = bf16 if t_len % BF16_SUBLANES == 0 else f32
    outs = pl.pallas_call(
        functools.partial(_mix_body, sb=sb, t_len=t_len),
        out_shape=[jax.ShapeDtypeStruct(x3d.shape, f32)] + [jax.ShapeDtypeStruct(a.shape, f32) for a in states],
        grid=grid,
        in_specs=([x_spec] + [seq_spec(a) for a in states] + [_layer_spec(w.shape, layer) for w in weights]
                  + [_const_spec(c.shape) for c in consts]),
        out_specs=[x_spec] + [seq_spec(a) for a in states],
        input_output_aliases={1 + k: 1 + k for k in range(N_STATES)},
        scratch_shapes=[
            pltpu.VMEM((sb, t_len + SUBLANES, LRU_W), f32),
            pltpu.VMEM((sb, t_len + SUBLANES, HW), f32),
            row_scratch(Z_W),
            row_scratch(HW, opnd),
            row_scratch(HW, opnd),
            row_scratch(HW, opnd),
            row_scratch(HW, opnd),
            row_scratch(ML_H * t_len),
            row_scratch(HW),
            row_scratch(HW),
            row_scratch(LANES),
            pltpu.VMEM((sb, 1, HW), f32),
            row_scratch(HW),
            row_scratch(GLA_KP),
            row_scratch(GLA_KP),
            pltpu.VMEM((sb * SUBLANES, GLA_KP), f32),
            row_scratch(ML_H * t_len),
            row_scratch(HW),
            row_scratch(LRU_W),
            row_scratch(HW),
            pltpu.VMEM((sb, HP, HW), f32),
            pltpu.VMEM((sb, GLA_KP, HP), f32),
        ],
        compiler_params=pltpu.CompilerParams(
            dimension_semantics=("arbitrary", "arbitrary"), vmem_limit_bytes=VMEM_LIMIT_BYTES),
        name="mixer",
    )(x3d, *states, *weights, *consts)
    return outs[0], list(outs[1:])


def _pad_last(w, width):
    return jnp.pad(w, [(0, 0)] * (w.ndim - 1) + [(0, width - w.shape[-1])])


def _pad_heads(w, d, dp):
    lead = w.shape[:-1]
    w = _pad_last(w.reshape(lead + (w.shape[-1] // d, d)), dp)
    return w.reshape(lead + (-1,))


def _pad_head_rows(w, d, dp):
    return jnp.swapaxes(_pad_heads(jnp.swapaxes(w, -1, -2), d, dp), -1, -2)


def _block_diag(blocks, dp=None):
    depth, n, d, _ = blocks.shape
    dp = d if dp is None else dp
    blocks = jnp.pad(blocks, ((0, 0), (0, 0), (0, dp - d), (0, dp - d)))
    eye = jnp.eye(n, dtype=blocks.dtype)
    return jnp.einsum("lhij,hg->lhigj", blocks, eye).reshape(depth, n * dp, n * dp)


def _vec_rows(v):
    v = v.astype(f32)
    v = v[:, None, :] if v.ndim == 2 else v
    return _pad_last(v, D_MODEL)


def _tail(conv_state, d=None, dp=None):
    conv_state = conv_state.astype(f32)
    if d is not None:
        conv_state = _pad_heads(conv_state, d, dp)
    return jnp.pad(conv_state, ((0, 0), (0, 0), (SUBLANES - (CONV_W - 1), 0), (0, 0)))


def _untail(tail, d=None, dp=None):
    tail = tail[:, :, SUBLANES - (CONV_W - 1):, :]
    if d is not None:
        tail = tail.reshape(tail.shape[:3] + (-1, dp))[..., :d].reshape(tail.shape[:3] + (-1,))
    return tail


def _mix_consts(t_len):
    head = jnp.arange(LANES)[:, None]
    lane = jnp.arange(HW)
    e = (head == (lane // HP)[None, :]).astype(bf16)
    e_rt = (head == (jnp.arange(ML_H * t_len) // t_len)[None, :]).astype(bf16)
    return jnp.concatenate([e, e], axis=0), jnp.concatenate([e_rt, e_rt], axis=0)


def kernel(x_prompt, x_sample, state_lru_h, state_lru_conv, state_mlstm_C, state_mlstm_n, state_mlstm_m, state_mlstm_conv, state_gla_S, meta_tokens, ffn1_norm_g, ffn1_w1, ffn1_w3, ffn1_w2, mix_norm_g, w_in, lru_conv_w, lru_conv_b, lru_wa, lru_ba, lru_wx, lru_bx, lru_lambda, ml_conv_w, ml_conv_b, ml_wq, ml_wk, ml_wv, ml_w_if, ml_b_if, ml_norm_g, ml_skip, gla_w_up, gla_b_up, gla_norm_g, w_out, ffn2_norm_g, ffn2_w1, ffn2_w3, ffn2_w2, final_norm_g):
    n_prompt = x_prompt.shape[0]
    n_sample, t_sample, _ = x_sample.shape

    sizes = (LRU_W, LRU_W, ML_W, ML_W, GLA_KW, GLA_KW, GLA_VW, GLA_VW, GLA_RANK)
    widths = (LRU_W, LRU_W, HW, HW, GLA_KP, GLA_KP, HW, HW, LANES)
    head_pads = (None, None, (ML_D, HP), (ML_D, HP), None, None, (GLA_DV, HP), (GLA_DV, HP), None)
    parts, off = [], 0
    for size, width, pad in zip(sizes, widths, head_pads):
        part = w_in[..., off:off + size]
        parts.append(_pad_last(part, width) if pad is None else _pad_heads(part, *pad))
        off += size
    wif = jnp.concatenate([_pad_head_rows(ml_w_if[:, i * ML_W:(i + 1) * ML_W], ML_D, HP) for i in range(3)],
                          axis=1)
    wif = jnp.concatenate([_pad_last(wif[..., :ML_H], LANES), _pad_last(wif[..., ML_H:], LANES)], axis=-1)
    bif = jnp.concatenate([_pad_last(ml_b_if[:, :ML_H], LANES), _pad_last(ml_b_if[:, ML_H:], LANES)], axis=-1)
    wo = jnp.concatenate([w_out[:, :LRU_W], _pad_head_rows(w_out[:, LRU_W:LRU_W + ML_W], ML_D, HP),
                          _pad_head_rows(w_out[:, LRU_W + ML_W:], GLA_DV, HP)], axis=1)
    ones_row = jnp.broadcast_to(((jnp.arange(HW) % HP) == ONE_LANE).astype(f32), (DEPTH, HW))
    vec_rows = [mix_norm_g, lru_conv_w, lru_conv_b, lru_ba, lru_bx, lru_lambda,
                _pad_heads(ml_conv_w, ML_D, HP), _pad_heads(ml_conv_b, ML_D, HP), bif,
                _pad_heads(ml_norm_g, ML_D, HP), _pad_heads(ml_skip, ML_D, HP),
                gla_b_up, _pad_heads(gla_norm_g, GLA_DV, HP), ones_row]
    vecs = jnp.concatenate([_vec_rows(v) for v in vec_rows], axis=1)
    vecs = jnp.pad(vecs, ((0, 0), (0, VEC_ROWS - vecs.shape[1]), (0, 0)))
    lw = dict(
        vecs=vecs,
        w_in=jnp.concatenate(parts, axis=-1).astype(bf16),
        lru_wa=_block_diag(lru_wa).astype(bf16), lru_wx=_block_diag(lru_wx).astype(bf16),
        ml_wq=_block_diag(ml_wq, HP).astype(bf16), ml_wk=_block_diag(ml_wk, HP).astype(bf16),
        ml_wv=_block_diag(ml_wv, HP).astype(bf16), ml_wif=wif.astype(bf16),
        gla_wup=jnp.pad(gla_w_up, ((0, 0), (0, LANES - GLA_RANK), (0, GLA_KP - GLA_KW))).astype(bf16),
        w_out=wo.astype(bf16))
    ffn1 = (ffn1_norm_g.astype(f32)[:, None, :], ffn1_w1.astype(bf16), ffn1_w3.astype(bf16), ffn1_w2.astype(bf16))
    ffn2 = (ffn2_norm_g.astype(f32)[:, None, :], ffn2_w1.astype(bf16), ffn2_w3.astype(bf16), ffn2_w2.astype(bf16))
    final_g = final_norm_g.astype(f32)[None, :]

    def run_trunk(x3d, states, *, sb, t_len, tm, need_output):
        n_seq, length, _ = x3d.shape
        consts = _mix_consts(t_len)
        x2d = _ffn(x3d.reshape(n_seq * length, D_MODEL), [(ffn1, 0)], final_g, tm=tm, post_norm=False)
        for l in range(DEPTH):
            x3d, states = _mix(x2d.reshape(n_seq, length, D_MODEL), states, lw, consts,
                               layer=l, sb=sb, t_len=t_len)
            x2d = x3d.reshape(n_seq * length, D_MODEL)
            if l < DEPTH - 1:
                x2d = _ffn(x2d, [(ffn2, l), (ffn1, l + 1)], final_g, tm=tm, post_norm=False)
            elif need_output:
                x2d = _ffn(x2d, [(ffn2, l)], final_g, tm=tm, post_norm=True)
        return x2d.reshape(n_seq, length, D_MODEL), states

    zeros = lambda *shape: jnp.zeros((DEPTH, n_prompt) + shape, f32)
    zero_states = [zeros(LRU_W), zeros(SUBLANES, LRU_W), zeros(SUBLANES, HW), zeros(ML_H, ML_D, ML_D),
                   zeros(ML_H, ML_D), zeros(1, LANES), zeros(GLA_H, GLA_DK, GLA_DV)]
    x_meta = jnp.broadcast_to(meta_tokens.astype(f32)[None], (n_prompt, N_META, D_MODEL))
    _, meta_states = run_trunk(x_meta, zero_states, sb=n_prompt, t_len=N_META, tm=n_prompt * N_META,
                               need_output=False)

    y_prompt, p_states = run_trunk(x_prompt, meta_states, sb=n_prompt, t_len=CHUNK, tm=FFN_ROWS, need_output=True)

    s_states = [state_lru_h.astype(f32), _tail(state_lru_conv), _tail(state_mlstm_conv, ML_D, HP),
                state_mlstm_C.astype(f32), state_mlstm_n.astype(f32),
                _pad_last(state_mlstm_m.astype(f32), LANES)[:, :, None, :], state_gla_S.astype(f32)]
    y_sample, s_states = run_trunk(x_sample, s_states, sb=SAMPLE_SEQS, t_len=t_sample, tm=FFN_ROWS,
                                   need_output=True)

    def reference_layout(st):
        lru_h, lru_tail, ml_tail, ml_c, ml_n, ml_m, gla_s = st
        return (lru_h, _untail(lru_tail), ml_c, ml_n, ml_m[:, :, 0, :ML_H], _untail(ml_tail, ML_D, HP),
                gla_s)

    return (y_prompt, y_sample) + reference_layout(p_states) + reference_layout(s_states)
```
